```python
import math
import jax
import jax.numpy as jnp
from jax import lax
import numpy as np

D_MODEL = 1024
BATCH = 16
SEQ = 2048
DEPTH = 4

N_MIXERS = 3

SSD_EXPAND = 2
SSD_D_INNER = SSD_EXPAND * D_MODEL
SSD_HEADDIM = 64
SSD_N_HEADS = SSD_D_INNER // SSD_HEADDIM
SSD_N_GROUPS = 4
SSD_HEADS_PER_GROUP = SSD_N_HEADS // SSD_N_GROUPS
SSD_D_STATE = 128
SSD_CONV_WIDTH = 4
SSD_CHUNK = 128
SSD_CONV_DIM = SSD_D_INNER + 2 * SSD_N_GROUPS * SSD_D_STATE
SSD_IN_DIM = SSD_D_INNER + SSD_CONV_DIM + SSD_N_HEADS

MOBA_HEAD_DIM = 64
MOBA_N_HEADS = D_MODEL // MOBA_HEAD_DIM
MOBA_BLOCK = 256
MOBA_TOPK = 3
MOBA_Q_BLOCK = 128

CONV_KERNEL = 31

PEER_N_KEYS = 128
PEER_N_EXPERTS = PEER_N_KEYS * PEER_N_KEYS
PEER_HEADS = 8
PEER_TOPK = 16
PEER_QUERY_DIM = 256
PEER_HALF = PEER_QUERY_DIM // 2
PEER_TOKEN_BLOCK = 128

PLE_DIM = 256

LN_EPS = 1e-5
DEEPNORM_ALPHA = (2 * DEPTH) ** 0.25
DEEPNORM_BETA = (8 * DEPTH) ** -0.25

kernel_name = 'hybrid_ssd_moba_conformer_peer_deepnorm'


def n_layers_of_kind(kind):
    return len(range(kind, DEPTH, N_MIXERS))


def layer_norm(x, g, b):
    xf = x.astype(jnp.float32)
    mu = jnp.mean(xf, axis=-1, keepdims=True)
    var = jnp.mean(jnp.square(xf - mu), axis=-1, keepdims=True)
    return ((xf - mu) * lax.rsqrt(var + LN_EPS)).astype(x.dtype) * g + b


def causal_depthwise_conv(x, w, b):
    k_width, chans = w.shape
    y = lax.conv_general_dilated(
        x, w[:, None, :].astype(x.dtype), window_strides=(1,), padding=[(k_width - 1, 0)],
        dimension_numbers=('NWC', 'WIO', 'NWC'), feature_group_count=chans)
    return y + b


def alibi_slopes(n_heads):
    return 2.0 ** (-8.0 * jnp.arange(1, n_heads + 1, dtype=jnp.float32) / n_heads)


def ssd_mixer(x, w_in, conv_w, conv_b, dt_bias, a_log, d_skip, norm_g, w_out):
    f32 = jnp.float32
    bsz, seq, _ = x.shape
    G, R, P, N, L = SSD_N_GROUPS, SSD_HEADS_PER_GROUP, SSD_HEADDIM, SSD_D_STATE, SSD_CHUNK
    nc = seq // L
    zxbcdt = x @ w_in
    z = zxbcdt[..., :SSD_D_INNER]
    xbc = zxbcdt[..., SSD_D_INNER:SSD_D_INNER + SSD_CONV_DIM]
    dt = zxbcdt[..., SSD_D_INNER + SSD_CONV_DIM:]
    xbc = jax.nn.silu(causal_depthwise_conv(xbc, conv_w, conv_b)).astype(f32)
    xs = xbc[..., :SSD_D_INNER]
    b_in = xbc[..., SSD_D_INNER:SSD_D_INNER + G * N].reshape(bsz, nc, L, G, N)
    c_in = xbc[..., SSD_D_INNER + G * N:].reshape(bsz, nc, L, G, N)
    dt = jax.nn.softplus(dt.astype(f32) + dt_bias.astype(f32)).reshape(bsz, nc, L, G, R)
    a = -jnp.exp(a_log.astype(f32)).reshape(G, R)
    x_dt = xs.reshape(bsz, nc, L, G, R, P) * dt[..., None]
    a_cum = jnp.cumsum(dt * a, axis=2)
    causal = jnp.tril(jnp.ones((L, L), dtype=bool))
    seg = a_cum[:, :, :, None] - a_cum[:, :, None, :]
    decay_ls = jnp.exp(jnp.where(causal[None, None, :, :, None, None], seg, -jnp.inf))
    cb = jnp.einsum('bclgn,bcsgn->bclsg', c_in, b_in)
    y_diag = jnp.einsum('bclsgr,bcsgrp->bclgrp', cb[..., None] * decay_ls, x_dt)
    decay_to_end = jnp.exp(a_cum[:, :, -1:] - a_cum)
    states = jnp.einsum('bclgn,bclgr,bclgrp->bcgrpn', b_in, decay_to_end, x_dt)
    chunk_decay = jnp.exp(a_cum[:, :, -1])

    def step(h, inp):
        st, dec = inp
        return dec[..., None, None] * h + st, h

    h0 = jnp.zeros((bsz, G, R, P, N), f32)
    _, prev = lax.scan(step, h0, (jnp.moveaxis(states, 1, 0), jnp.moveaxis(chunk_decay, 1, 0)))
    y_off = jnp.einsum('bclgn,cbgrpn,bclgr->bclgrp', c_in, prev, jnp.exp(a_cum))
    y = (y_diag + y_off).reshape(bsz, seq, SSD_N_HEADS, P)
    y = y + d_skip.astype(f32)[:, None] * xs.reshape(bsz, seq, SSD_N_HEADS, P)
    gsz = SSD_D_INNER // G
    gated = y.reshape(bsz, seq, G, gsz) * jax.nn.silu(z.astype(f32)).reshape(bsz, seq, G, gsz)
    gated = gated * lax.rsqrt(jnp.mean(jnp.square(gated), axis=-1, keepdims=True) + LN_EPS)
    y = (gated.reshape(bsz, seq, SSD_D_INNER) * norm_g.astype(f32)).astype(x.dtype)
    return y @ w_out


def moba_attention(x, w_qkv, w_out):
    f32 = jnp.float32
    bsz, seq, _ = x.shape
    H, Dh = MOBA_N_HEADS, MOBA_HEAD_DIM
    n_blk = -(-seq // MOBA_BLOCK)
    pad = n_blk * MOBA_BLOCK - seq
    k_sel = min(MOBA_TOPK, n_blk)
    n_qblk = seq // MOBA_Q_BLOCK
    qkv = (x @ w_qkv).reshape(bsz, seq, 3, H, Dh)
    q = qkv[:, :, 0].transpose(0, 2, 1, 3)
    padw = ((0, 0), (0, 0), (0, pad), (0, 0))
    k = jnp.pad(qkv[:, :, 1].transpose(0, 2, 1, 3), padw)
    v = jnp.pad(qkv[:, :, 2].transpose(0, 2, 1, 3), padw)
    k_blocks = k.reshape(bsz, H, n_blk, MOBA_BLOCK, Dh)
    v_blocks = v.reshape(bsz, H, n_blk, MOBA_BLOCK, Dh)
    k_mean = jnp.mean(k_blocks, axis=3)
    slopes = alibi_slopes(H)
    scale = Dh ** -0.5
    offs = jnp.arange(MOBA_BLOCK)
    blk_ids = jnp.arange(n_blk)

    def attend_one_sequence(args):
        q_s, kb, vb, km = args

        def attend_query_block(qi):
            q0 = qi * MOBA_Q_BLOCK
            qb = lax.dynamic_slice_in_dim(q_s, q0, MOBA_Q_BLOCK, axis=1)
            t = q0 + jnp.arange(MOBA_Q_BLOCK)
            own = q0 // MOBA_BLOCK
            gate = jnp.einsum('hqd,hnd->hqn', qb, km).astype(f32)
            gate = jnp.where(blk_ids < own, gate, -jnp.inf)
            _, sel = lax.top_k(gate, k_sel)
            valid = sel < own
            k_g = jax.vmap(lambda kbh, ih: kbh[ih])(kb, sel)
            v_g = jax.vmap(lambda vbh, ih: vbh[ih])(vb, sel)
            s_pos = sel[..., None] * MOBA_BLOCK + offs
            logit_sel = (jnp.einsum('hqd,hqjkd->hqjk', qb, k_g).astype(f32) * scale
                         - slopes[:, None, None, None] * (t[None, :, None, None] - s_pos).astype(f32))
            logit_sel = jnp.where(valid[..., None], logit_sel, -jnp.inf)
            k_own = lax.dynamic_index_in_dim(kb, own, axis=1, keepdims=False)
            v_own = lax.dynamic_index_in_dim(vb, own, axis=1, keepdims=False)
            o_pos = own * MOBA_BLOCK + offs
            dist = (t[:, None] - o_pos[None, :]).astype(f32)
            logit_own = (jnp.einsum('hqd,hkd->hqk', qb, k_own).astype(f32) * scale
                         - slopes[:, None, None] * dist[None])
            logit_own = jnp.where((o_pos[None, :] <= t[:, None])[None], logit_own, -jnp.inf)
            n_sel = k_sel * MOBA_BLOCK
            probs = jax.nn.softmax(
                jnp.concatenate([logit_sel.reshape(H, MOBA_Q_BLOCK, n_sel), logit_own], axis=-1), axis=-1)
            p_sel = probs[..., :n_sel].reshape(H, MOBA_Q_BLOCK, k_sel, MOBA_BLOCK).astype(v_g.dtype)
            p_own = probs[..., n_sel:].astype(v_own.dtype)
            return (jnp.einsum('hqjk,hqjkd->hqd', p_sel, v_g)
                    + jnp.einsum('hqk,hkd->hqd', p_own, v_own))

        out = lax.map(attend_query_block, jnp.arange(n_qblk))
        return out.transpose(1, 0, 2, 3).reshape(H, seq, Dh)

    out = lax.map(attend_one_sequence, (q, k_blocks, v_blocks, k_mean))
    out = out.transpose(0, 2, 1, 3).reshape(bsz, seq, D_MODEL).astype(x.dtype)
    return out @ w_out


def conformer_conv_module(x, w_pw1, b_pw1, w_dw, b_dw, ln_g, ln_b, w_pw2):
    h = x @ w_pw1 + b_pw1
    h = h[..., :D_MODEL] * jax.nn.sigmoid(h[..., D_MODEL:])
    h = causal_depthwise_conv(h, w_dw, b_dw)
    h = jax.nn.silu(layer_norm(h, ln_g, ln_b))
    return h @ w_pw2


def peer_ffn(x, w_q, sub_keys, u, v):
    f32 = jnp.float32
    bsz, seq, d = x.shape
    n_tok = bsz * seq
    xt = x.reshape(n_tok // PEER_TOKEN_BLOCK, PEER_TOKEN_BLOCK, d)

    def block(xb):
        q = (xb @ w_q).reshape(PEER_TOKEN_BLOCK, PEER_HEADS, 2, PEER_HALF)
        s = jnp.einsum('thcd,hcnd->thcn', q, sub_keys).astype(f32)
        v1, i1 = lax.top_k(s[:, :, 0], PEER_TOPK)
        v2, i2 = lax.top_k(s[:, :, 1], PEER_TOPK)
        cand = (v1[..., :, None] + v2[..., None, :]).reshape(PEER_TOKEN_BLOCK, PEER_HEADS, PEER_TOPK * PEER_TOPK)
        cidx = (i1[..., :, None] * PEER_N_KEYS + i2[..., None, :]).reshape(PEER_TOKEN_BLOCK, PEER_HEADS, PEER_TOPK * PEER_TOPK)
        best, pos = lax.top_k(cand, PEER_TOPK)
        expert = jnp.take_along_axis(cidx, pos, axis=-1)
        gate = jax.nn.softmax(best, axis=-1)
        u_g = u[expert]
        v_g = v[expert]
        act = jax.nn.gelu(jnp.einsum('td,thkd->thk', xb, u_g).astype(f32))
        return jnp.einsum('thk,thkd->td', (gate * act).astype(v_g.dtype), v_g)

    return lax.map(block, xt).reshape(bsz, seq, d)


def setup_inputs(seed: int = 0) -> dict:
    key = jax.random.key(seed)
    ks = iter(jax.random.split(key, 40))
    f32 = jnp.float32

    def nrm(shape, std):
        return jax.random.normal(next(ks), shape, f32) * std

    n_a, n_b, n_c = n_layers_of_kind(0), n_layers_of_kind(1), n_layers_of_kind(2)
    beta = DEEPNORM_BETA
    dt0 = jnp.exp(jax.random.uniform(next(ks), (n_a, SSD_N_HEADS), f32)
                  * (math.log(0.1) - math.log(0.001)) + math.log(0.001))
    dt0 = jnp.maximum(dt0, 1e-4)
    inp = {}
    inp['x'] = nrm((BATCH, SEQ, D_MODEL), 1.0)
    inp['p'] = nrm((DEPTH, BATCH, SEQ, PLE_DIM), 1.0)
    inp['ssd_w_in'] = nrm((n_a, D_MODEL, SSD_IN_DIM), D_MODEL ** -0.5)
    inp['ssd_conv_w'] = nrm((n_a, SSD_CONV_WIDTH, SSD_CONV_DIM), SSD_CONV_WIDTH ** -0.5)
    inp['ssd_conv_b'] = nrm((n_a, SSD_CONV_DIM), 0.02)
    inp['ssd_dt_bias'] = dt0 + jnp.log(-jnp.expm1(-dt0))
    inp['ssd_a_log'] = jnp.log(jax.random.uniform(next(ks), (n_a, SSD_N_HEADS), f32, 1.0, 16.0))
    inp['ssd_d'] = 1.0 + nrm((n_a, SSD_N_HEADS), 0.02)
    inp['ssd_norm_g'] = 1.0 + nrm((n_a, SSD_D_INNER), 0.02)
    inp['ssd_w_out'] = nrm((n_a, SSD_D_INNER, D_MODEL), beta * SSD_D_INNER ** -0.5)
    inp['moba_w_qkv'] = nrm((n_b, D_MODEL, 3 * D_MODEL), D_MODEL ** -0.5)
    inp['moba_w_out'] = nrm((n_b, D_MODEL, D_MODEL), beta * D_MODEL ** -0.5)
    inp['conv_w_pw1'] = nrm((n_c, D_MODEL, 2 * D_MODEL), D_MODEL ** -0.5)
    inp['conv_b_pw1'] = nrm((n_c, 2 * D_MODEL), 0.02)
    inp['conv_w_dw'] = nrm((n_c, CONV_KERNEL, D_MODEL), CONV_KERNEL ** -0.5)
    inp['conv_b_dw'] = nrm((n_c, D_MODEL), 0.02)
    inp['conv_ln_g'] = 1.0 + nrm((n_c, D_MODEL), 0.02)
    inp['conv_ln_b'] = nrm((n_c, D_MODEL), 0.02)
    inp['conv_w_pw2'] = nrm((n_c, D_MODEL, D_MODEL), beta * D_MODEL ** -0.5)
    inp['peer_w_q'] = nrm((DEPTH, D_MODEL, PEER_HEADS * PEER_QUERY_DIM), D_MODEL ** -0.5)
    inp['peer_sub_keys'] = nrm((DEPTH, PEER_HEADS, 2, PEER_N_KEYS, PEER_HALF), PEER_HALF ** -0.5)
    inp['peer_u'] = nrm((DEPTH, PEER_N_EXPERTS, D_MODEL), D_MODEL ** -0.5)
    inp['peer_v'] = nrm((DEPTH, PEER_N_EXPERTS, D_MODEL), beta * PEER_HEADS ** -0.5)
    inp['ln_mix_g'] = 1.0 + nrm((DEPTH, D_MODEL), 0.02)
    inp['ln_mix_b'] = nrm((DEPTH, D_MODEL), 0.02)
    inp['ln_ffn_g'] = 1.0 + nrm((DEPTH, D_MODEL), 0.02)
    inp['ln_ffn_b'] = nrm((DEPTH, D_MODEL), 0.02)
    inp['ple_w_gate'] = nrm((DEPTH, D_MODEL, D_MODEL), D_MODEL ** -0.5)
    inp['ple_w_proj'] = nrm((DEPTH, PLE_DIM, D_MODEL), PLE_DIM ** -0.5)
    return inp


def reference(x, p, ssd_w_in, ssd_conv_w, ssd_conv_b, ssd_dt_bias, ssd_a_log, ssd_d, ssd_norm_g,
              ssd_w_out, moba_w_qkv, moba_w_out, conv_w_pw1, conv_b_pw1, conv_w_dw, conv_b_dw,
              conv_ln_g, conv_ln_b, conv_w_pw2, peer_w_q, peer_sub_keys, peer_u, peer_v,
              ln_mix_g, ln_mix_b, ln_ffn_g, ln_ffn_b, ple_w_gate, ple_w_proj):
    alpha = DEEPNORM_ALPHA
    for i in range(DEPTH):
        kind, j = i % N_MIXERS, i // N_MIXERS
        if kind == 0:
            mix = ssd_mixer(x, ssd_w_in[j], ssd_conv_w[j], ssd_conv_b[j], ssd_dt_bias[j], ssd_a_log[j],
                            ssd_d[j], ssd_norm_g[j], ssd_w_out[j])
        elif kind == 1:
            mix = moba_attention(x, moba_w_qkv[j], moba_w_out[j])
        else:
            mix = conformer_conv_module(x, conv_w_pw1[j], conv_b_pw1[j], conv_w_dw[j], conv_b_dw[j],
                                        conv_ln_g[j], conv_ln_b[j], conv_w_pw2[j])
        x = layer_norm(alpha * x + mix, ln_mix_g[i], ln_mix_b[i])
        ffn = peer_ffn(x, peer_w_q[i], peer_sub_keys[i], peer_u[i], peer_v[i])
        x = layer_norm(alpha * x + ffn, ln_ffn_g[i], ln_ffn_b[i])
        gate = jax.nn.sigmoid((x @ ple_w_gate[i]).astype(jnp.float32)).astype(x.dtype)
        x = x + gate * (p[i] @ ple_w_proj[i])
    return x
```

```python
import functools
import math

import jax
import jax.numpy as jnp
from jax import lax
from jax.experimental import pallas as pl
from jax.experimental.pallas import tpu as pltpu

F32 = jnp.float32
BF16 = jnp.bfloat16

D_MODEL = 1024
DEPTH = 4
N_MIXERS = 3

SSD_D_INNER = 2048
SSD_HEADDIM = 64
SSD_N_HEADS = 32
SSD_N_GROUPS = 4
SSD_HEADS_PER_GROUP = 8
SSD_D_STATE = 128
SSD_CONV_WIDTH = 4
SSD_CHUNK = 128
SSD_GROUP_LANES = SSD_D_INNER // SSD_N_GROUPS
SSD_BC_DIM = 2 * SSD_N_GROUPS * SSD_D_STATE

MOBA_HEAD_DIM = 64
MOBA_N_HEADS = 16
MOBA_BLOCK = 256
MOBA_TOPK = 3
MOBA_Q_BLOCK = 128

CONV_KERNEL = 31
CONV_PAD_ROWS = 32

PEER_N_KEYS = 128
PEER_HEADS = 8
PEER_TOPK = 16
PEER_QUERY_DIM = 256
PEER_HALF = 128
PEER_PICKS = PEER_HEADS * PEER_TOPK

LN_EPS = 1e-5
DEEPNORM_ALPHA = (2 * DEPTH) ** 0.25

LANES = 128
VMEM_LIMIT = 48 * 1024 * 1024

NEG_INF = float("-inf")


def _params(sem):
    return pltpu.CompilerParams(dimension_semantics=sem, vmem_limit_bytes=VMEM_LIMIT)


def _layer_norm(x, g, b):
    mu = jnp.mean(x, axis=-1, keepdims=True)
    xc = x - mu
    var = jnp.mean(xc * xc, axis=-1, keepdims=True)
    return xc * lax.rsqrt(var + LN_EPS) * g + b


def _sigmoid(x):
    return 1.0 / (1.0 + jnp.exp(-x))


def _silu(x):
    return x * _sigmoid(x)


def _dot(a, b):
    return jnp.dot(a.astype(BF16), b.astype(BF16), preferred_element_type=F32)


def _dot_nt(a, b):
    return lax.dot_general(a.astype(BF16), b.astype(BF16), (((1,), (1,)), ((), ())),
                           preferred_element_type=F32)


def _dot_exact(a, b):
    return jnp.dot(a, b, preferred_element_type=F32, precision=lax.Precision.HIGHEST)


def _mm_kernel(a_ref, w_ref, o_ref):
    o_ref[...] = _dot(a_ref[...], w_ref[...])


def matmul(a, w, tm=512, tn=512):
    m, k = a.shape
    n = w.shape[1]
    tm, tn = min(tm, m), min(tn, n)
    return pl.pallas_call(
        _mm_kernel,
        grid=(n // tn, m // tm),
        in_specs=[pl.BlockSpec((tm, k), lambda j, i: (i, 0)),
                  pl.BlockSpec((k, tn), lambda j, i: (0, j))],
        out_specs=pl.BlockSpec((tm, tn), lambda j, i: (i, j)),
        out_shape=jax.ShapeDtypeStruct((m, n), F32),
        name="mm",
        compiler_params=_params(("parallel", "parallel")),
    )(a, w)


def _mm_glu_kernel(a_ref, w1_ref, w2_ref, b1_ref, b2_ref, o_ref):
    a = a_ref[...].astype(BF16)
    h1 = jnp.dot(a, w1_ref[...], preferred_element_type=F32) + b1_ref[...]
    h2 = jnp.dot(a, w2_ref[...], preferred_element_type=F32) + b2_ref[...]
    o_ref[...] = h1 * _sigmoid(h2)


def matmul_glu(a, w1, w2, b1, b2, tm=512, tn=512):
    m, k = a.shape
    n = w1.shape[1]
    tm, tn = min(tm, m), min(tn, n)
    wspec = pl.BlockSpec((k, tn), lambda j, i: (0, j))
    bspec = pl.BlockSpec((1, tn), lambda j, i: (0, j))
    return pl.pallas_call(
        _mm_glu_kernel,
        grid=(n // tn, m // tm),
        in_specs=[pl.BlockSpec((tm, k), lambda j, i: (i, 0)), wspec, wspec, bspec, bspec],
        out_specs=pl.BlockSpec((tm, tn), lambda j, i: (i, j)),
        out_shape=jax.ShapeDtypeStruct((m, n), F32),
        name="mm_glu",
        compiler_params=_params(("parallel", "parallel")),
    )(a, w1, w2, b1, b2)


def _mm_ln_kernel(a_ref, w_ref, res_ref, g_ref, b_ref, o_ref):
    mix = _dot(a_ref[...], w_ref[...])
    o_ref[...] = _layer_norm(DEEPNORM_ALPHA * res_ref[...] + mix, g_ref[...], b_ref[...])


def _mm_lnsilu_ln_kernel(a_ref, pg_ref, pb_ref, w_ref, res_ref, g_ref, b_ref, o_ref):
    h = _silu(_layer_norm(a_ref[...], pg_ref[...], pb_ref[...]))
    mix = _dot(h, w_ref[...])
    o_ref[...] = _layer_norm(DEEPNORM_ALPHA * res_ref[...] + mix, g_ref[...], b_ref[...])


def matmul_ln(a, w, res, g, b, pre=None, tm=256):
    m, k = a.shape
    n = w.shape[1]
    tm = min(tm, m)
    row = lambda width: pl.BlockSpec((1, width), lambda i: (0, 0))
    a_spec = pl.BlockSpec((tm, k), lambda i: (i, 0))
    w_spec = pl.BlockSpec((k, n), lambda i: (0, 0))
    io_spec = pl.BlockSpec((tm, n), lambda i: (i, 0))
    if pre is None:
        body, ins, args = _mm_ln_kernel, [a_spec, w_spec, io_spec, row(n), row(n)], (a, w, res, g, b)
    else:
        body = _mm_lnsilu_ln_kernel
        ins = [a_spec, row(k), row(k), w_spec, io_spec, row(n), row(n)]
        args = (a, pre[0], pre[1], w, res, g, b)
    return pl.pallas_call(
        body, grid=(m // tm,), in_specs=ins, out_specs=io_spec,
        out_shape=jax.ShapeDtypeStruct((m, n), F32),
        name="mm_ln",
        compiler_params=_params(("parallel",)),
    )(*args)


def _ple_kernel(x_ref, p_ref, wg_ref, wp_ref, xres_ref, o_ref):
    gate = _sigmoid(_dot(x_ref[...], wg_ref[...]))
    o_ref[...] = xres_ref[...] + gate * _dot(p_ref[...], wp_ref[...])


def ple_add(x, p, wg, wp, tm=512, tn=512):
    m, k = x.shape
    kp = p.shape[1]
    n = wg.shape[1]
    tm, tn = min(tm, m), min(tn, n)
    return pl.pallas_call(
        _ple_kernel,
        grid=(n // tn, m // tm),
        in_specs=[pl.BlockSpec((tm, k), lambda j, i: (i, 0)),
                  pl.BlockSpec((tm, kp), lambda j, i: (i, 0)),
                  pl.BlockSpec((k, tn), lambda j, i: (0, j)),
                  pl.BlockSpec((kp, tn), lambda j, i: (0, j)),
                  pl.BlockSpec((tm, tn), lambda j, i: (i, j))],
        out_specs=pl.BlockSpec((tm, tn), lambda j, i: (i, j)),
        out_shape=jax.ShapeDtypeStruct((m, n), F32),
        name="ple_add",
        compiler_params=_params(("parallel", "parallel")),
    )(x, p, wg, wp, x)


def _ssd_kernel(z_ref, x_ref, bc_ref, dt_ref, cwx_ref, cwbc_ref, cbx_ref, cbbc_ref, dtb_ref, alog_ref,
                dexp_ref, ng_ref, expand_ref, o_ref, xpad, bcpad, state, yacc):
    L = SSD_CHUNK
    first = pl.program_id(1) == 0

    @pl.when(first)
    def _():
        xpad[0:8, :] = jnp.zeros((8, SSD_D_INNER), F32)
        bcpad[0:8, :] = jnp.zeros((8, SSD_BC_DIM), F32)
        state[...] = jnp.zeros_like(state)

    xpad[8:8 + L, :] = x_ref[...]
    bcpad[8:8 + L, :] = bc_ref[...]
    xs = cbx_ref[...]
    bc = cbbc_ref[...]
    for k in range(SSD_CONV_WIDTH):
        lo = 8 - (SSD_CONV_WIDTH - 1) + k
        xs = xs + cwx_ref[k:k + 1, :] * xpad[lo:lo + L, :]
        bc = bc + cwbc_ref[k:k + 1, :] * bcpad[lo:lo + L, :]
    xs = _silu(xs)
    bc = _silu(bc)
    xpad[0:8, :] = x_ref[L - 8:L, :]
    bcpad[0:8, :] = bc_ref[L - 8:L, :]

    lane = lax.broadcasted_iota(jnp.int32, (1, LANES), 1)
    dt_in = dt_ref[...] + dtb_ref[...]
    dt = jnp.maximum(dt_in, 0.0) + jnp.log1p(jnp.exp(-jnp.abs(dt_in)))
    dt = jnp.where(lane < SSD_N_HEADS, dt, 0.0)
    a = -jnp.exp(alog_ref[...])
    row = lax.broadcasted_iota(jnp.int32, (L, L), 0)
    col = lax.broadcasted_iota(jnp.int32, (L, L), 1)
    causal = col <= row
    a_cum = _dot_exact(causal.astype(F32), dt * a)
    a_cum_t = a_cum.T
    expand = expand_ref[...]
    dt_e = _dot_exact(dt, expand)
    acum_e = _dot_exact(a_cum, expand)
    alast_e = acum_e[L - 1:L, :]
    x_dt = xs * dt_e
    x_end = jnp.exp(alast_e - acum_e) * x_dt
    decay_in = jnp.exp(acum_e)

    lane_l = lax.broadcasted_iota(jnp.int32, (L, LANES), 1)
    for g in range(SSD_N_GROUPS):
        gl = slice(g * SSD_GROUP_LANES, (g + 1) * SSD_GROUP_LANES)
        b_g = bc[:, g * SSD_D_STATE:(g + 1) * SSD_D_STATE]
        c_g = bc[:, (SSD_N_GROUPS + g) * SSD_D_STATE:(SSD_N_GROUPS + g + 1) * SSD_D_STATE]
        cb = _dot_nt(c_g, b_g)
        prev = state[:, gl]
        yacc[:, gl] = _dot(c_g, prev) * decay_in[:, gl]
        new_states = _dot(b_g.T, x_end[:, gl])
        state[:, gl] = jnp.exp(alast_e[:, gl]) * prev + new_states
        for pr in range(SSD_HEADS_PER_GROUP // 2):
            ms = []
            for sub in range(2):
                h = g * SSD_HEADS_PER_GROUP + 2 * pr + sub
                seg = a_cum[:, h:h + 1] - a_cum_t[h:h + 1, :]
                ms.append(cb * jnp.exp(jnp.where(causal, seg, NEG_INF)))
            m_cat = jnp.concatenate(ms, axis=1)
            pl_ = slice(g * SSD_GROUP_LANES + pr * LANES, g * SSD_GROUP_LANES + (pr + 1) * LANES)
            xp = x_dt[:, pl_]
            x_bd = jnp.concatenate([jnp.where(lane_l < SSD_HEADDIM, xp, 0.0),
                                    jnp.where(lane_l >= SSD_HEADDIM, xp, 0.0)], axis=0)
            yacc[:, pl_] = yacc[:, pl_] + _dot(m_cat, x_bd)

    y = yacc[...] + dexp_ref[...] * xs
    gated = y * _silu(z_ref[...])
    for g in range(SSD_N_GROUPS):
        gl = slice(g * SSD_GROUP_LANES, (g + 1) * SSD_GROUP_LANES)
        gg = gated[:, gl]
        ms = jnp.mean(gg * gg, axis=-1, keepdims=True)
        o_ref[:, gl] = gg * lax.rsqrt(ms + LN_EPS) * ng_ref[:, gl]


def ssd_core(zxbc, dtp, conv_w, conv_b, dt_bias, a_log, d_skip, norm_g, bsz, seq):
    L = SSD_CHUNK
    nc = seq // L
    pad_h = LANES - SSD_N_HEADS
    head_of_lane = jnp.arange(SSD_D_INNER, dtype=jnp.int32) // SSD_HEADDIM
    expand = (jnp.arange(LANES, dtype=jnp.int32)[:, None] == head_of_lane[None, :]).astype(F32)
    dexp = jnp.repeat(d_skip.astype(F32), SSD_HEADDIM)[None, :]
    dtb = jnp.pad(dt_bias.astype(F32), (0, pad_h))[None, :]
    alog = jnp.pad(a_log.astype(F32), (0, pad_h))[None, :]
    cwx, cwbc = conv_w[:, :SSD_D_INNER], conv_w[:, SSD_D_INNER:]
    cbx, cbbc = conv_b[None, :SSD_D_INNER], conv_b[None, SSD_D_INNER:]
    chunk = lambda b, c: b * nc + c
    full = lambda shape: pl.BlockSpec(shape, lambda b, c: (0, 0))
    return pl.pallas_call(
        _ssd_kernel,
        grid=(bsz, nc),
        in_specs=[pl.BlockSpec((L, SSD_D_INNER), lambda b, c: (chunk(b, c), 0)),
                  pl.BlockSpec((L, SSD_D_INNER), lambda b, c: (chunk(b, c), 1)),
                  pl.BlockSpec((L, SSD_BC_DIM), lambda b, c: (chunk(b, c), 4)),
                  pl.BlockSpec((L, LANES), lambda b, c: (chunk(b, c), 0)),
                  full((SSD_CONV_WIDTH, SSD_D_INNER)), full((SSD_CONV_WIDTH, SSD_BC_DIM)),
                  full((1, SSD_D_INNER)), full((1, SSD_BC_DIM)),
                  full((1, LANES)), full((1, LANES)),
                  full((1, SSD_D_INNER)), full((1, SSD_D_INNER)),
                  full((LANES, SSD_D_INNER))],
        out_specs=pl.BlockSpec((L, SSD_D_INNER), lambda b, c: (chunk(b, c), 0)),
        out_shape=jax.ShapeDtypeStruct((bsz * seq, SSD_D_INNER), F32),
        scratch_shapes=[pltpu.VMEM((L + 8, SSD_D_INNER), F32),
                        pltpu.VMEM((L + 8, SSD_BC_DIM), F32),
                        pltpu.VMEM((SSD_D_STATE, SSD_D_INNER), F32),
                        pltpu.VMEM((L, SSD_D_INNER), F32)],
        name="ssd_core",
        compiler_params=_params(("parallel", "arbitrary")),
    )(zxbc, zxbc, zxbc, dtp, cwx, cwbc, cbx, cbbc, dtb, alog, dexp, norm_g[None, :], expand)


def _moba_kernel(slopes_ref, q_ref, k_ref, v_ref, o_ref, kmean):
    hp = pl.program_id(1)
    qi = pl.program_id(2)
    Q, KB, DH = MOBA_Q_BLOCK, MOBA_BLOCK, MOBA_HEAD_DIM
    n_blk = k_ref.shape[0] // KB
    own = (qi * Q) // KB
    q0 = qi * Q
    scale = DH ** -0.5

    @pl.when(qi == 0)
    def _():
        kmean[...] = jnp.zeros_like(kmean)
        for n in range(n_blk):
            kmean[n:n + 1, :] = jnp.mean(k_ref[n * KB:(n + 1) * KB, :], axis=0, keepdims=True)

    lane = lax.broadcasted_iota(jnp.int32, (Q, LANES), 1)
    rc = (lax.broadcasted_iota(jnp.int32, (Q, KB), 0) - lax.broadcasted_iota(jnp.int32, (Q, KB), 1))
    own_start = pl.multiple_of(own * KB, KB)
    outs = []
    for sub in range(2):
        ls = slice(sub * DH, (sub + 1) * DH)
        slope = slopes_ref[2 * hp + sub]
        q = q_ref[:, ls]
        gate = jnp.where(lane < own, _dot_nt(q, kmean[:, ls]), NEG_INF)
        rank = jnp.zeros((Q, LANES), jnp.int32)
        for m in range(n_blk):
            gm = gate[:, m:m + 1]
            beats = (gm > gate) | ((gm == gate) & (lane > m))
            rank = rank + beats.astype(jnp.int32)
        sel = ((rank < MOBA_TOPK) & (lane < own)).astype(F32)

        def logits(k_blk, dist):
            return _dot_nt(q, k_blk) * scale - slope * dist.astype(F32)

        dist = rc + (q0 - own * KB)
        lg = jnp.where(dist >= 0, logits(k_ref[pl.ds(own_start, KB), ls], dist), NEG_INF)
        m_run = jnp.max(lg, axis=-1, keepdims=True)
        p = jnp.exp(lg - m_run)
        l_run = jnp.sum(p, axis=-1, keepdims=True)
        acc = _dot(p, v_ref[pl.ds(own_start, KB), ls])

        def body(n, carry):
            m_run, l_run, acc = carry
            start = pl.multiple_of(n * KB, KB)
            picked = jnp.sum(jnp.where(lane == n, sel, 0.0), axis=-1, keepdims=True)
            lg = logits(k_ref[pl.ds(start, KB), ls], rc + (q0 - n * KB))
            lg = jnp.where(picked > 0.0, lg, NEG_INF)
            m_new = jnp.maximum(m_run, jnp.max(lg, axis=-1, keepdims=True))
            alpha = jnp.exp(m_run - m_new)
            p = jnp.exp(lg - m_new)
            l_new = alpha * l_run + jnp.sum(p, axis=-1, keepdims=True)
            acc_new = alpha * acc + _dot(p, v_ref[pl.ds(start, KB), ls])
            return m_new, l_new, acc_new

        m_run, l_run, acc = lax.fori_loop(0, own, body, (m_run, l_run, acc))
        outs.append(acc / l_run)
    o_ref[...] = jnp.concatenate(outs, axis=1)


def moba_core(qkv, bsz, seq):
    Q = MOBA_Q_BLOCK
    nq = seq // Q
    n_pairs = MOBA_N_HEADS // 2
    slopes = 2.0 ** (-8.0 * jnp.arange(1, MOBA_N_HEADS + 1, dtype=F32) / MOBA_N_HEADS)
    grid_spec = pltpu.PrefetchScalarGridSpec(
        num_scalar_prefetch=1,
        grid=(bsz, n_pairs, nq),
        in_specs=[pl.BlockSpec((Q, LANES), lambda b, h, i, s: (b * nq + i, h)),
                  pl.BlockSpec((seq, LANES), lambda b, h, i, s: (b, n_pairs + h)),
                  pl.BlockSpec((seq, LANES), lambda b, h, i, s: (b, 2 * n_pairs + h))],
        out_specs=pl.BlockSpec((Q, LANES), lambda b, h, i, s: (b * nq + i, h)),
        scratch_shapes=[pltpu.VMEM((LANES, LANES), F32)],
    )
    return pl.pallas_call(
        _moba_kernel, grid_spec=grid_spec,
        out_shape=jax.ShapeDtypeStruct((bsz * seq, D_MODEL), F32),
        name="moba_core",
        compiler_params=_params(("parallel", "parallel", "arbitrary")),
    )(slopes, qkv, qkv, qkv)


CONV_ROWS = 256


def _dwconv_kernel(h_ref, w_ref, b_ref, o_ref, pad):
    seq = h_ref.shape[1]
    pad[0:CONV_PAD_ROWS, :] = jnp.zeros((CONV_PAD_ROWS, pad.shape[1]), F32)
    pad[CONV_PAD_ROWS:CONV_PAD_ROWS + seq, :] = h_ref[0]
    first = CONV_PAD_ROWS - (CONV_KERNEL - 1)
    for c in range(seq // CONV_ROWS):
        r0 = c * CONV_ROWS
        acc = jnp.broadcast_to(b_ref[...], (CONV_ROWS, pad.shape[1]))
        for k in range(CONV_KERNEL):
            acc = acc + w_ref[k:k + 1, :] * pad[r0 + first + k:r0 + first + k + CONV_ROWS, :]
        o_ref[0, r0:r0 + CONV_ROWS, :] = acc


def dwconv(h, w, b, bsz, seq, tc=256):
    chans = h.shape[1]
    h3 = h.reshape(bsz, seq, chans)
    out = pl.pallas_call(
        _dwconv_kernel,
        grid=(bsz, chans // tc),
        in_specs=[pl.BlockSpec((1, seq, tc), lambda b, c: (b, 0, c)),
                  pl.BlockSpec((CONV_KERNEL, tc), lambda b, c: (0, c)),
                  pl.BlockSpec((1, tc), lambda b, c: (0, c))],
        out_specs=pl.BlockSpec((1, seq, tc), lambda b, c: (b, 0, c)),
        out_shape=jax.ShapeDtypeStruct((bsz, seq, chans), F32),
        scratch_shapes=[pltpu.VMEM((CONV_PAD_ROWS + seq, tc), F32)],
        name="dwconv",
        compiler_params=_params(("parallel", "parallel")),
    )(h3, w, b[None, :])
    return out.reshape(bsz * seq, chans)


def _top16_rows(cur, dst_val, dst_idx, payload=None):
    rows = cur.shape[0]
    riota = lax.broadcasted_iota(jnp.int32, cur.shape, 0)
    for i in range(PEER_TOPK):
        m = jnp.max(cur, axis=0, keepdims=True)
        am = jnp.min(jnp.where(cur == m, riota, rows), axis=0, keepdims=True)
        hit = riota == am
        dst_val[i:i + 1, :] = m
        if payload is None:
            dst_idx[i:i + 1, :] = am
        else:
            dst_idx[i:i + 1, :] = jnp.max(jnp.where(hit, payload, -1), axis=0, keepdims=True)
        cur = jnp.where(hit, NEG_INF, cur)


def _peer_route_kernel(x_ref, wq_ref, sk_ref, eidx_ref, gate_ref, v1, i1, v2, i2, cand, cidx, best):
    K = PEER_TOPK
    q = _dot(x_ref[...], wq_ref[...])
    for h in range(PEER_HEADS):
        for c, (vv, ii) in enumerate(((v1, i1), (v2, i2))):
            hc = 2 * h + c
            s_t = _dot_nt(sk_ref[hc], q[:, hc * PEER_HALF:(hc + 1) * PEER_HALF])
            _top16_rows(s_t, vv, ii)
        for a in range(K):
            cand[a * K:(a + 1) * K, :] = v1[a:a + 1, :] + v2[...]
            cidx[a * K:(a + 1) * K, :] = i1[a:a + 1, :] * PEER_N_KEYS + i2[...]
        _top16_rows(cand[...], best, eidx_ref.at[h * K:(h + 1) * K, :], payload=cidx[...])
        b = best[...]
        e = jnp.exp(b - b[0:1, :])
        gate_ref[h * K:(h + 1) * K, :] = e / jnp.sum(e, axis=0, keepdims=True)


def peer_route(x, wq, sk, tm=256):
    t, d = x.shape
    tm = min(tm, t)
    K = PEER_TOPK
    out_spec = pl.BlockSpec((PEER_PICKS, tm), lambda i: (0, i))
    return pl.pallas_call(
        _peer_route_kernel,
        grid=(t // tm,),
        in_specs=[pl.BlockSpec((tm, d), lambda i: (i, 0)),
                  pl.BlockSpec(wq.shape, lambda i: (0, 0)),
                  pl.BlockSpec(sk.shape, lambda i: (0, 0, 0))],
        out_specs=[out_spec, out_spec],
        out_shape=[jax.ShapeDtypeStruct((PEER_PICKS, t), jnp.int32),
                   jax.ShapeDtypeStruct((PEER_PICKS, t), F32)],
        scratch_shapes=[pltpu.VMEM((K, tm), F32), pltpu.VMEM((K, tm), jnp.int32),
                        pltpu.VMEM((K, tm), F32), pltpu.VMEM((K, tm), jnp.int32),
                        pltpu.VMEM((K * K, tm), F32), pltpu.VMEM((K * K, tm), jnp.int32),
                        pltpu.VMEM((K, tm), F32)],
        name="peer_route",
        compiler_params=_params(("parallel",)),
    )(x, wq, sk)


PEER_TOKENS_PER_STEP = 8


def _gelu_tanh(x):
    return 0.5 * x * (1.0 + jnp.tanh(math.sqrt(2.0 / math.pi) * (x + 0.044715 * (x * x * x))))


def _peer_expert_kernel(idx_ref, nidx_ref, gate_ref, x_ref, g_ref, b_ref, uv_hbm, o_ref, buf, sem, ffn):
    TB, P = PEER_TOKENS_PER_STEP, PEER_PICKS
    i = pl.program_id(0)
    n = pl.num_programs(0)
    slot = i % 2

    def row_copy(ids, t, j, slot_):
        return pltpu.make_async_copy(uv_hbm.at[pl.ds(ids[t, j], 1), :],
                                     buf.at[slot_, pl.ds(t * P + j, 1), :], sem.at[slot_])

    def issue(ids, slot_):
        for t in range(TB):
            for j in range(P):
                row_copy(ids, t, j, slot_).start()

    @pl.when(i == 0)
    def _():
        issue(idx_ref, 0)

    @pl.when(i + 1 < n)
    def _():
        issue(nidx_ref, 1 - slot)

    pltpu.make_async_copy(uv_hbm.at[pl.ds(0, TB * P), :], buf.at[slot], sem.at[slot]).wait()

    for t in range(TB):
        rows = buf.at[slot, t * P:(t + 1) * P, :]
        xt = x_ref[t:t + 1, :]
        act = jnp.sum(rows[:, 0:D_MODEL] * xt, axis=-1, keepdims=True)
        wgt = gate_ref[0, :, t:t + 1] * _gelu_tanh(act)
        ffn[t:t + 1, :] = jnp.sum(rows[:, D_MODEL:2 * D_MODEL] * wgt, axis=0, keepdims=True)
    o_ref[...] = _layer_norm(DEEPNORM_ALPHA * x_ref[...] + ffn[...], g_ref[...], b_ref[...])


def peer_experts(x, eidx_t, gates, uv, g, b):
    t, d = x.shape
    TB, P = PEER_TOKENS_PER_STEP, PEER_PICKS
    n = t // TB
    gates = gates.reshape(P, n, TB).transpose(1, 0, 2)
    row = pl.BlockSpec((1, d), lambda i: (0, 0))
    return pl.pallas_call(
        _peer_expert_kernel,
        grid=(n,),
        in_specs=[pl.BlockSpec((TB, P), lambda i: (i, 0), memory_space=pltpu.SMEM),
                  pl.BlockSpec((TB, P), lambda i: (jnp.minimum(i + 1, n - 1), 0), memory_space=pltpu.SMEM),
                  pl.BlockSpec((1, P, TB), lambda i: (i, 0, 0)),
                  pl.BlockSpec((TB, d), lambda i: (i, 0)),
                  row, row,
                  pl.BlockSpec(memory_space=pl.ANY)],
        out_specs=pl.BlockSpec((TB, d), lambda i: (i, 0)),
        out_shape=jax.ShapeDtypeStruct((t, d), F32),
        scratch_shapes=[pltpu.VMEM((2, TB * P, 2 * d), F32),
                        pltpu.SemaphoreType.DMA((2,)),
                        pltpu.VMEM((TB, d), F32)],
        name="peer_experts",
        compiler_params=_params(("arbitrary",)),
    )(eidx_t, eidx_t, gates, x, g, b, uv)


def _row(v):
    return v.astype(F32)[None, :]


def kernel(x, p, ssd_w_in, ssd_conv_w, ssd_conv_b, ssd_dt_bias, ssd_a_log, ssd_d, ssd_norm_g, ssd_w_out, moba_w_qkv, moba_w_out, conv_w_pw1, conv_b_pw1, conv_w_dw, conv_b_dw, conv_ln_g, conv_ln_b, conv_w_pw2, peer_w_q, peer_sub_keys, peer_u, peer_v, ln_mix_g, ln_mix_b, ln_ffn_g, ln_ffn_b, ple_w_gate, ple_w_proj):
    bsz, seq, d = x.shape
    t = bsz * seq
    x = x.reshape(t, d)
    zx_end = SSD_D_INNER + SSD_D_INNER + SSD_BC_DIM
    for i in range(DEPTH):
        kind, j = i % N_MIXERS, i // N_MIXERS
        g_mix, b_mix = _row(ln_mix_g[i]), _row(ln_mix_b[i])
        if kind == 0:
            w_in = ssd_w_in[j]
            w_dt = jnp.pad(w_in[:, zx_end:], ((0, 0), (0, LANES - SSD_N_HEADS)))
            zxbc = matmul(x, w_in[:, :zx_end].astype(BF16))
            dtp = matmul(x, w_dt.astype(BF16))
            y = ssd_core(zxbc, dtp, ssd_conv_w[j], ssd_conv_b[j], ssd_dt_bias[j], ssd_a_log[j], ssd_d[j],
                         ssd_norm_g[j], bsz, seq)
            x = matmul_ln(y, ssd_w_out[j].astype(BF16), x, g_mix, b_mix)
        elif kind == 1:
            qkv = matmul(x, moba_w_qkv[j].astype(BF16))
            att = moba_core(qkv, bsz, seq)
            x = matmul_ln(att, moba_w_out[j].astype(BF16), x, g_mix, b_mix)
        else:
            w1 = conv_w_pw1[j].astype(BF16)
            h = matmul_glu(x, w1[:, :D_MODEL], w1[:, D_MODEL:],
                           _row(conv_b_pw1[j][:D_MODEL]), _row(conv_b_pw1[j][D_MODEL:]))
            h = dwconv(h, conv_w_dw[j], conv_b_dw[j], bsz, seq)
            x = matmul_ln(h, conv_w_pw2[j].astype(BF16), x, g_mix, b_mix,
                          pre=(_row(conv_ln_g[j]), _row(conv_ln_b[j])))
        sk = peer_sub_keys[i].reshape(PEER_HEADS * 2, PEER_N_KEYS, PEER_HALF).astype(BF16)
        eidx, gates = peer_route(x, peer_w_q[i].astype(BF16), sk)
        uv = jnp.concatenate([peer_u[i], peer_v[i]], axis=1)
        x = peer_experts(x, eidx.T, gates, uv, _row(ln_ffn_g[i]), _row(ln_ffn_b[i]))
        x = ple_add(x, p[i].reshape(t, -1), ple_w_gate[i].astype(BF16), ple_w_proj[i].astype(BF16))
    return x.reshape(bsz, seq, d)
```

```python
import functools
import math

import jax
import jax.numpy as jnp
from jax import lax
from jax.experimental import pallas as pl
from jax.experimental.pallas import tpu as pltpu

F32 = jnp.float32
BF16 = jnp.bfloat16

D_MODEL = 1024
DEPTH = 4
N_MIXERS = 3

SSD_D_INNER = 2048
SSD_HEADDIM = 64
SSD_N_HEADS = 32
SSD_N_GROUPS = 4
SSD_HEADS_PER_GROUP = 8
SSD_D_STATE = 128
SSD_CONV_WIDTH = 4
SSD_CHUNK = 128
SSD_GROUP_LANES = SSD_D_INNER // SSD_N_GROUPS
SSD_BC_DIM = 2 * SSD_N_GROUPS * SSD_D_STATE

MOBA_HEAD_DIM = 64
MOBA_N_HEADS = 16
MOBA_BLOCK = 256
MOBA_TOPK = 3
MOBA_Q_BLOCK = 128

CONV_KERNEL = 31
CONV_PAD_ROWS = 32

PEER_N_KEYS = 128
PEER_HEADS = 8
PEER_TOPK = 16
PEER_QUERY_DIM = 256
PEER_HALF = 128
PEER_PICKS = PEER_HEADS * PEER_TOPK

LN_EPS = 1e-5
DEEPNORM_ALPHA = (2 * DEPTH) ** 0.25

LANES = 128
VMEM_LIMIT = 48 * 1024 * 1024

NEG_INF = float("-inf")


def _params(sem):
    return pltpu.CompilerParams(dimension_semantics=sem, vmem_limit_bytes=VMEM_LIMIT)


def _layer_norm(x, g, b):
    mu = jnp.mean(x, axis=-1, keepdims=True)
    xc = x - mu
    var = jnp.mean(xc * xc, axis=-1, keepdims=True)
    return xc * lax.rsqrt(var + LN_EPS) * g + b


def _sigmoid(x):
    return 1.0 / (1.0 + jnp.exp(-x))


def _silu(x):
    return x * _sigmoid(x)


def _dot(a, b):
    return jnp.dot(a.astype(BF16), b.astype(BF16), preferred_element_type=F32)


def _dot_nt(a, b):
    return lax.dot_general(a.astype(BF16), b.astype(BF16), (((1,), (1,)), ((), ())),
                           preferred_element_type=F32)


def _dot_exact(a, b):
    return jnp.dot(a, b, preferred_element_type=F32, precision=lax.Precision.HIGHEST)


def _mm_kernel(a_ref, w_ref, o_ref):
    o_ref[...] = _dot(a_ref[...], w_ref[...])


def matmul(a, w, tm=512, tn=512):
    m, k = a.shape
    n = w.shape[1]
    tm, tn = min(tm, m), min(tn, n)
    return pl.pallas_call(
        _mm_kernel,
        grid=(n // tn, m // tm),
        in_specs=[pl.BlockSpec((tm, k), lambda j, i: (i, 0)),
                  pl.BlockSpec((k, tn), lambda j, i: (0, j))],
        out_specs=pl.BlockSpec((tm, tn), lambda j, i: (i, j)),
        out_shape=jax.ShapeDtypeStruct((m, n), F32),
        name="mm",
        compiler_params=_params(("parallel", "parallel")),
    )(a, w)


def _mm_glu_kernel(a_ref, w1_ref, w2_ref, b1_ref, b2_ref, o_ref):
    a = a_ref[...].astype(BF16)
    h1 = jnp.dot(a, w1_ref[...], preferred_element_type=F32) + b1_ref[...]
    h2 = jnp.dot(a, w2_ref[...], preferred_element_type=F32) + b2_ref[...]
    o_ref[...] = h1 * _sigmoid(h2)


def matmul_glu(a, w1, w2, b1, b2, tm=512, tn=512):
    m, k = a.shape
    n = w1.shape[1]
    tm, tn = min(tm, m), min(tn, n)
    wspec = pl.BlockSpec((k, tn), lambda j, i: (0, j))
    bspec = pl.BlockSpec((1, tn), lambda j, i: (0, j))
    return pl.pallas_call(
        _mm_glu_kernel,
        grid=(n // tn, m // tm),
        in_specs=[pl.BlockSpec((tm, k), lambda j, i: (i, 0)), wspec, wspec, bspec, bspec],
        out_specs=pl.BlockSpec((tm, tn), lambda j, i: (i, j)),
        out_shape=jax.ShapeDtypeStruct((m, n), F32),
        name="mm_glu",
        compiler_params=_params(("parallel", "parallel")),
    )(a, w1, w2, b1, b2)


def _mm_ln_kernel(a_ref, w_ref, res_ref, g_ref, b_ref, o_ref):
    mix = _dot(a_ref[...], w_ref[...])
    o_ref[...] = _layer_norm(DEEPNORM_ALPHA * res_ref[...] + mix, g_ref[...], b_ref[...])


def _mm_lnsilu_ln_kernel(a_ref, pg_ref, pb_ref, w_ref, res_ref, g_ref, b_ref, o_ref):
    h = _silu(_layer_norm(a_ref[...], pg_ref[...], pb_ref[...]))
    mix = _dot(h, w_ref[...])
    o_ref[...] = _layer_norm(DEEPNORM_ALPHA * res_ref[...] + mix, g_ref[...], b_ref[...])


def matmul_ln(a, w, res, g, b, pre=None, tm=256):
    m, k = a.shape
    n = w.shape[1]
    tm = min(tm, m)
    row = lambda width: pl.BlockSpec((1, width), lambda i: (0, 0))
    a_spec = pl.BlockSpec((tm, k), lambda i: (i, 0))
    w_spec = pl.BlockSpec((k, n), lambda i: (0, 0))
    io_spec = pl.BlockSpec((tm, n), lambda i: (i, 0))
    if pre is None:
        body, ins, args = _mm_ln_kernel, [a_spec, w_spec, io_spec, row(n), row(n)], (a, w, res, g, b)
    else:
        body = _mm_lnsilu_ln_kernel
        ins = [a_spec, row(k), row(k), w_spec, io_spec, row(n), row(n)]
        args = (a, pre[0], pre[1], w, res, g, b)
    return pl.pallas_call(
        body, grid=(m // tm,), in_specs=ins, out_specs=io_spec,
        out_shape=jax.ShapeDtypeStruct((m, n), F32),
        name="mm_ln",
        compiler_params=_params(("parallel",)),
    )(*args)


def _ple_kernel(x_ref, p_ref, wg_ref, wp_ref, xres_ref, o_ref):
    gate = _sigmoid(_dot(x_ref[...], wg_ref[...]))
    o_ref[...] = xres_ref[...] + gate * _dot(p_ref[...], wp_ref[...])


def ple_add(x, p, wg, wp, tm=512, tn=512):
    m, k = x.shape
    kp = p.shape[1]
    n = wg.shape[1]
    tm, tn = min(tm, m), min(tn, n)
    return pl.pallas_call(
        _ple_kernel,
        grid=(n // tn, m // tm),
        in_specs=[pl.BlockSpec((tm, k), lambda j, i: (i, 0)),
                  pl.BlockSpec((tm, kp), lambda j, i: (i, 0)),
                  pl.BlockSpec((k, tn), lambda j, i: (0, j)),
                  pl.BlockSpec((kp, tn), lambda j, i: (0, j)),
                  pl.BlockSpec((tm, tn), lambda j, i: (i, j))],
        out_specs=pl.BlockSpec((tm, tn), lambda j, i: (i, j)),
        out_shape=jax.ShapeDtypeStruct((m, n), F32),
        name="ple_add",
        compiler_params=_params(("parallel", "parallel")),
    )(x, p, wg, wp, x)


def _ssd_kernel(z_ref, x_ref, bc_ref, dt_ref, cwx_ref, cwbc_ref, cbx_ref, cbbc_ref, dtb_ref, alog_ref,
                dexp_ref, ng_ref, expand_ref, o_ref, xpad, bcpad, state, yacc):
    L = SSD_CHUNK
    first = pl.program_id(1) == 0

    @pl.when(first)
    def _():
        xpad[0:8, :] = jnp.zeros((8, SSD_D_INNER), F32)
        bcpad[0:8, :] = jnp.zeros((8, SSD_BC_DIM), F32)
        state[...] = jnp.zeros_like(state)

    xpad[8:8 + L, :] = x_ref[...]
    bcpad[8:8 + L, :] = bc_ref[...]
    xs = cbx_ref[...]
    bc = cbbc_ref[...]
    for k in range(SSD_CONV_WIDTH):
        lo = 8 - (SSD_CONV_WIDTH - 1) + k
        xs = xs + cwx_ref[k:k + 1, :] * xpad[lo:lo + L, :]
        bc = bc + cwbc_ref[k:k + 1, :] * bcpad[lo:lo + L, :]
    xs = _silu(xs)
    bc = _silu(bc)
    xpad[0:8, :] = x_ref[L - 8:L, :]
    bcpad[0:8, :] = bc_ref[L - 8:L, :]

    lane = lax.broadcasted_iota(jnp.int32, (1, LANES), 1)
    dt_in = dt_ref[...] + dtb_ref[...]
    dt = jnp.maximum(dt_in, 0.0) + jnp.log1p(jnp.exp(-jnp.abs(dt_in)))
    dt = jnp.where(lane < SSD_N_HEADS, dt, 0.0)
    a = -jnp.exp(alog_ref[...])
    row = lax.broadcasted_iota(jnp.int32, (L, L), 0)
    col = lax.broadcasted_iota(jnp.int32, (L, L), 1)
    causal = col <= row
    a_cum = _dot_exact(causal.astype(F32), dt * a)
    a_cum_t = a_cum.T
    expand = expand_ref[...]
    dt_e = _dot_exact(dt, expand)
    acum_e = _dot_exact(a_cum, expand)
    alast_e = acum_e[L - 1:L, :]
    x_dt = xs * dt_e
    x_end = jnp.exp(alast_e - acum_e) * x_dt
    decay_in = jnp.exp(acum_e)

    lane_l = lax.broadcasted_iota(jnp.int32, (L, LANES), 1)
    for g in range(SSD_N_GROUPS):
        gl = slice(g * SSD_GROUP_LANES, (g + 1) * SSD_GROUP_LANES)
        b_g = bc[:, g * SSD_D_STATE:(g + 1) * SSD_D_STATE]
        c_g = bc[:, (SSD_N_GROUPS + g) * SSD_D_STATE:(SSD_N_GROUPS + g + 1) * SSD_D_STATE]
        cb = _dot_nt(c_g, b_g)
        prev = state[:, gl]
        yacc[:, gl] = _dot(c_g, prev) * decay_in[:, gl]
        new_states = _dot(b_g.T, x_end[:, gl])
        state[:, gl] = jnp.exp(alast_e[:, gl]) * prev + new_states
        for pr in range(SSD_HEADS_PER_GROUP // 2):
            ms = []
            for sub in range(2):
                h = g * SSD_HEADS_PER_GROUP + 2 * pr + sub
                seg = a_cum[:, h:h + 1] - a_cum_t[h:h + 1, :]
                ms.append(cb * jnp.exp(jnp.where(causal, seg, NEG_INF)))
            m_cat = jnp.concatenate(ms, axis=1)
            pl_ = slice(g * SSD_GROUP_LANES + pr * LANES, g * SSD_GROUP_LANES + (pr + 1) * LANES)
            xp = x_dt[:, pl_]
            x_bd = jnp.concatenate([jnp.where(lane_l < SSD_HEADDIM, xp, 0.0),
                                    jnp.where(lane_l >= SSD_HEADDIM, xp, 0.0)], axis=0)
            yacc[:, pl_] = yacc[:, pl_] + _dot(m_cat, x_bd)

    y = yacc[...] + dexp_ref[...] * xs
    gated = y * _silu(z_ref[...])
    for g in range(SSD_N_GROUPS):
        gl = slice(g * SSD_GROUP_LANES, (g + 1) * SSD_GROUP_LANES)
        gg = gated[:, gl]
        ms = jnp.mean(gg * gg, axis=-1, keepdims=True)
        o_ref[:, gl] = gg * lax.rsqrt(ms + LN_EPS) * ng_ref[:, gl]


def ssd_core(zxbc, dtp, conv_w, conv_b, dt_bias, a_log, d_skip, norm_g, bsz, seq):
    L = SSD_CHUNK
    nc = seq // L
    pad_h = LANES - SSD_N_HEADS
    head_of_lane = jnp.arange(SSD_D_INNER, dtype=jnp.int32) // SSD_HEADDIM
    expand = (jnp.arange(LANES, dtype=jnp.int32)[:, None] == head_of_lane[None, :]).astype(F32)
    dexp = jnp.repeat(d_skip.astype(F32), SSD_HEADDIM)[None, :]
    dtb = jnp.pad(dt_bias.astype(F32), (0, pad_h))[None, :]
    alog = jnp.pad(a_log.astype(F32), (0, pad_h))[None, :]
    cwx, cwbc = conv_w[:, :SSD_D_INNER], conv_w[:, SSD_D_INNER:]
    cbx, cbbc = conv_b[None, :SSD_D_INNER], conv_b[None, SSD_D_INNER:]
    chunk = lambda b, c: b * nc + c
    full = lambda shape: pl.BlockSpec(shape, lambda b, c: (0, 0))
    return pl.pallas_call(
        _ssd_kernel,
        grid=(bsz, nc),
        in_specs=[pl.BlockSpec((L, SSD_D_INNER), lambda b, c: (chunk(b, c), 0)),
                  pl.BlockSpec((L, SSD_D_INNER), lambda b, c: (chunk(b, c), 1)),
                  pl.BlockSpec((L, SSD_BC_DIM), lambda b, c: (chunk(b, c), 4)),
                  pl.BlockSpec((L, LANES), lambda b, c: (chunk(b, c), 0)),
                  full((SSD_CONV_WIDTH, SSD_D_INNER)), full((SSD_CONV_WIDTH, SSD_BC_DIM)),
                  full((1, SSD_D_INNER)), full((1, SSD_BC_DIM)),
                  full((1, LANES)), full((1, LANES)),
                  full((1, SSD_D_INNER)), full((1, SSD_D_INNER)),
                  full((LANES, SSD_D_INNER))],
        out_specs=pl.BlockSpec((L, SSD_D_INNER), lambda b, c: (chunk(b, c), 0)),
        out_shape=jax.ShapeDtypeStruct((bsz * seq, SSD_D_INNER), F32),
        scratch_shapes=[pltpu.VMEM((L + 8, SSD_D_INNER), F32),
                        pltpu.VMEM((L + 8, SSD_BC_DIM), F32),
                        pltpu.VMEM((SSD_D_STATE, SSD_D_INNER), F32),
                        pltpu.VMEM((L, SSD_D_INNER), F32)],
        name="ssd_core",
        compiler_params=_params(("parallel", "arbitrary")),
    )(zxbc, zxbc, zxbc, dtp, cwx, cwbc, cbx, cbbc, dtb, alog, dexp, norm_g[None, :], expand)


def _moba_kernel(slopes_ref, q_ref, k_ref, v_ref, o_ref, kmean):
    hp = pl.program_id(1)
    qi = pl.program_id(2)
    Q, KB, DH = MOBA_Q_BLOCK, MOBA_BLOCK, MOBA_HEAD_DIM
    n_blk = k_ref.shape[0] // KB
    own = (qi * Q) // KB
    q0 = qi * Q
    scale = DH ** -0.5

    @pl.when(qi == 0)
    def _():
        kmean[...] = jnp.zeros_like(kmean)
        for n in range(n_blk):
            kmean[n:n + 1, :] = jnp.mean(k_ref[n * KB:(n + 1) * KB, :], axis=0, keepdims=True)

    lane = lax.broadcasted_iota(jnp.int32, (Q, LANES), 1)
    rc = (lax.broadcasted_iota(jnp.int32, (Q, KB), 0) - lax.broadcasted_iota(jnp.int32, (Q, KB), 1))
    own_start = pl.multiple_of(own * KB, KB)
    heads = []
    for sub in range(2):
        ls = slice(sub * DH, (sub + 1) * DH)
        q = q_ref[:, ls]
        gate = jnp.where(lane < own, _dot_nt(q, kmean[:, ls]), NEG_INF)
        rank = jnp.zeros((Q, LANES), jnp.int32)
        for m in range(n_blk):
            gm = gate[:, m:m + 1]
            beats = (gm > gate) | ((gm == gate) & (lane > m))
            rank = rank + beats.astype(jnp.int32)
        sel = ((rank < MOBA_TOPK) & (lane < own)).astype(F32)
        heads.append((ls, slopes_ref[2 * hp + sub], q, sel))

    def logits(hd, start, dist):
        ls, slope, q, _ = hd
        return _dot_nt(q, k_ref[pl.ds(start, KB), ls]) * scale - slope * dist

    dist_own = rc + (q0 - own * KB)
    dist_own_f = dist_own.astype(F32)
    carry = []
    for hd in heads:
        lg = jnp.where(dist_own >= 0, logits(hd, own_start, dist_own_f), NEG_INF)
        m_run = jnp.max(lg, axis=-1, keepdims=True)
        p = jnp.exp(lg - m_run)
        carry += [m_run, jnp.sum(p, axis=-1, keepdims=True), _dot(p, v_ref[pl.ds(own_start, KB), hd[0]])]

    def body(n, carry):
        start = pl.multiple_of(n * KB, KB)
        dist_f = (rc + (q0 - n * KB)).astype(F32)
        out = []
        for i, hd in enumerate(heads):
            m_run, l_run, acc = carry[3 * i:3 * i + 3]
            picked = jnp.sum(jnp.where(lane == n, hd[3], 0.0), axis=-1, keepdims=True)
            lg = jnp.where(picked > 0.0, logits(hd, start, dist_f), NEG_INF)
            m_new = jnp.maximum(m_run, jnp.max(lg, axis=-1, keepdims=True))
            alpha = jnp.exp(m_run - m_new)
            p = jnp.exp(lg - m_new)
            out += [m_new, alpha * l_run + jnp.sum(p, axis=-1, keepdims=True),
                    alpha * acc + _dot(p, v_ref[pl.ds(start, KB), hd[0]])]
        return tuple(out)

    carry = lax.fori_loop(0, own, body, tuple(carry))
    o_ref[...] = jnp.concatenate([carry[2] / carry[1], carry[5] / carry[4]], axis=1)


def moba_core(qkv, bsz, seq):
    Q = MOBA_Q_BLOCK
    nq = seq // Q
    n_pairs = MOBA_N_HEADS // 2
    slopes = 2.0 ** (-8.0 * jnp.arange(1, MOBA_N_HEADS + 1, dtype=F32) / MOBA_N_HEADS)
    grid_spec = pltpu.PrefetchScalarGridSpec(
        num_scalar_prefetch=1,
        grid=(bsz, n_pairs, nq),
        in_specs=[pl.BlockSpec((Q, LANES), lambda b, h, i, s: (b * nq + i, h)),
                  pl.BlockSpec((seq, LANES), lambda b, h, i, s: (b, n_pairs + h)),
                  pl.BlockSpec((seq, LANES), lambda b, h, i, s: (b, 2 * n_pairs + h))],
        out_specs=pl.BlockSpec((Q, LANES), lambda b, h, i, s: (b * nq + i, h)),
        scratch_shapes=[pltpu.VMEM((LANES, LANES), F32)],
    )
    return pl.pallas_call(
        _moba_kernel, grid_spec=grid_spec,
        out_shape=jax.ShapeDtypeStruct((bsz * seq, D_MODEL), F32),
        name="moba_core",
        compiler_params=_params(("parallel", "parallel", "arbitrary")),
    )(slopes, qkv, qkv, qkv)


CONV_ROWS = 256


def _dwconv_kernel(h_ref, w_ref, b_ref, o_ref, pad):
    seq = h_ref.shape[1]
    pad[0:CONV_PAD_ROWS, :] = jnp.zeros((CONV_PAD_ROWS, pad.shape[1]), F32)
    pad[CONV_PAD_ROWS:CONV_PAD_ROWS + seq, :] = h_ref[0]
    first = CONV_PAD_ROWS - (CONV_KERNEL - 1)
    for c in range(seq // CONV_ROWS):
        r0 = c * CONV_ROWS
        acc = jnp.broadcast_to(b_ref[...], (CONV_ROWS, pad.shape[1]))
        for k in range(CONV_KERNEL):
            acc = acc + w_ref[k:k + 1, :] * pad[r0 + first + k:r0 + first + k + CONV_ROWS, :]
        o_ref[0, r0:r0 + CONV_ROWS, :] = acc


def dwconv(h, w, b, bsz, seq, tc=256):
    chans = h.shape[1]
    h3 = h.reshape(bsz, seq, chans)
    out = pl.pallas_call(
        _dwconv_kernel,
        grid=(bsz, chans // tc),
        in_specs=[pl.BlockSpec((1, seq, tc), lambda b, c: (b, 0, c)),
                  pl.BlockSpec((CONV_KERNEL, tc), lambda b, c: (0, c)),
                  pl.BlockSpec((1, tc), lambda b, c: (0, c))],
        out_specs=pl.BlockSpec((1, seq, tc), lambda b, c: (b, 0, c)),
        out_shape=jax.ShapeDtypeStruct((bsz, seq, chans), F32),
        scratch_shapes=[pltpu.VMEM((CONV_PAD_ROWS + seq, tc), F32)],
        name="dwconv",
        compiler_params=_params(("parallel", "parallel")),
    )(h3, w, b[None, :])
    return out.reshape(bsz * seq, chans)


PEER_CAND_PER_A = tuple(PEER_TOPK // (a + 1) for a in range(PEER_TOPK))
PEER_CAND_ROWS = 56


def _top16_rows(cur, dst_val, dst_idx, payload=None):
    rows = cur.shape[0]
    riota = lax.broadcasted_iota(jnp.int32, cur.shape, 0).astype(F32)
    for i in range(PEER_TOPK):
        m = jnp.max(cur, axis=0, keepdims=True)
        am = jnp.min(jnp.where(cur == m, riota, float(rows)), axis=0, keepdims=True)
        hit = riota == am
        dst_val[i:i + 1, :] = m
        if payload is None:
            dst_idx[i:i + 1, :] = am
        else:
            dst_idx[i:i + 1, :] = jnp.max(jnp.where(hit, payload, -1.0), axis=0, keepdims=True)
        cur = jnp.where(hit, NEG_INF, cur)


def _peer_route_kernel(x_ref, wq_ref, sk_ref, eidx_ref, gate_ref, v1, i1, v2, i2, cand, cidx, best, bidx):
    K = PEER_TOPK
    q = _dot(x_ref[...], wq_ref[...])
    cand[PEER_CAND_ROWS - 8:PEER_CAND_ROWS, :] = jnp.full((8, cand.shape[1]), NEG_INF, F32)
    cidx[PEER_CAND_ROWS - 8:PEER_CAND_ROWS, :] = jnp.zeros((8, cand.shape[1]), F32)
    for h in range(PEER_HEADS):
        for c, (vv, ii) in enumerate(((v1, i1), (v2, i2))):
            hc = 2 * h + c
            s_t = _dot_nt(sk_ref[hc], q[:, hc * PEER_HALF:(hc + 1) * PEER_HALF])
            _top16_rows(s_t, vv, ii)
        off = 0
        for a, n_b in enumerate(PEER_CAND_PER_A):
            cand[off:off + n_b, :] = v1[a:a + 1, :] + v2[0:n_b, :]
            cidx[off:off + n_b, :] = i1[a:a + 1, :] * float(PEER_N_KEYS) + i2[0:n_b, :]
            off += n_b
        _top16_rows(cand[...], best, bidx, payload=cidx[...])
        eidx_ref[h * K:(h + 1) * K, :] = bidx[...].astype(jnp.int32)
        b = best[...]
        e = jnp.exp(b - b[0:1, :])
        gate_ref[h * K:(h + 1) * K, :] = e / jnp.sum(e, axis=0, keepdims=True)


def peer_route(x, wq, sk, tm=256):
    t, d = x.shape
    tm = min(tm, t)
    K = PEER_TOPK
    out_spec = pl.BlockSpec((PEER_PICKS, tm), lambda i: (0, i))
    return pl.pallas_call(
        _peer_route_kernel,
        grid=(t // tm,),
        in_specs=[pl.BlockSpec((tm, d), lambda i: (i, 0)),
                  pl.BlockSpec(wq.shape, lambda i: (0, 0)),
                  pl.BlockSpec(sk.shape, lambda i: (0, 0, 0))],
        out_specs=[out_spec, out_spec],
        out_shape=[jax.ShapeDtypeStruct((PEER_PICKS, t), jnp.int32),
                   jax.ShapeDtypeStruct((PEER_PICKS, t), F32)],
        scratch_shapes=[pltpu.VMEM((K, tm), F32)] * 4
                       + [pltpu.VMEM((PEER_CAND_ROWS, tm), F32)] * 2
                       + [pltpu.VMEM((K, tm), F32)] * 2,
        name="peer_route",
        compiler_params=_params(("parallel",)),
    )(x, wq, sk)


PEER_STEP_TOKENS = 64
PEER_GROUP = 8
CHUNKS = D_MODEL // LANES
WORD_ROWS = CHUNKS // 2
PEER_VMEM_LIMIT = 44 * 1024 * 1024


def _gelu_tanh(x):
    return 0.5 * x * (1.0 + jnp.tanh(math.sqrt(2.0 / math.pi) * (x + 0.044715 * (x * x * x))))


def pack_expert_table(w):
    n = w.shape[0]
    bits = lax.bitcast_convert_type(w.astype(jnp.bfloat16), jnp.uint16).astype(jnp.uint32)
    bits = bits.reshape(n, WORD_ROWS, 2, LANES)
    return lax.bitcast_convert_type(bits[:, :, 0, :] | (bits[:, :, 1, :] << 16), jnp.int32)


def _load_table(tab_hbm, tab, sem):
    @pl.when(pl.program_id(0) == 0)
    def _():
        cp = pltpu.make_async_copy(tab_hbm, tab, sem)
        cp.start()
        cp.wait()


def _gather_rows(idx_ref, t, tab, stage):
    for j in range(PEER_PICKS):
        stage[j * WORD_ROWS:(j + 1) * WORD_ROWS, :] = tab[idx_ref[t, j]]
    return pltpu.bitcast(stage[...], jnp.bfloat16)


def _chunk_mask():
    sub = lax.broadcasted_iota(jnp.int32, (CHUNKS, PEER_PICKS * CHUNKS), 0)
    lane = lax.broadcasted_iota(jnp.int32, (CHUNKS, PEER_PICKS * CHUNKS), 1)
    return sub == (lane & (CHUNKS - 1))


def _peer_act_kernel(idx_ref, x_ref, gate_ref, tab_hbm, w_ref, tab, stage0, stage1, acts, sem):
    _load_table(tab_hbm, tab, sem)
    mask = _chunk_mask()
    lane = lax.broadcasted_iota(jnp.int32, (PEER_GROUP, LANES), 1)

    def group(g, carry):
        base = pl.multiple_of(g * PEER_GROUP, PEER_GROUP)
        for tt in range(PEER_GROUP):
            t = base + tt
            rows = _gather_rows(idx_ref, t, tab, stage0 if tt % 2 == 0 else stage1)
            r = _dot_nt(x_ref[t], rows)
            acts[tt:tt + 1, :] = jnp.sum(jnp.where(mask, r, 0.0), axis=0, keepdims=True)
        for c in range(CHUNKS):
            a = acts[:, c * LANES:(c + 1) * LANES]
            for k in (1, 2, 4):
                a = a + jnp.where((lane & k) != 0, pltpu.roll(a, k, axis=1), pltpu.roll(a, LANES - k, axis=1))
            sl = (pl.ds(base, PEER_GROUP), slice(c * LANES, (c + 1) * LANES))
            w_ref[sl] = gate_ref[sl] * _gelu_tanh(a)
        return carry

    lax.fori_loop(0, PEER_STEP_TOKENS // PEER_GROUP, group, 0)


def _peer_mix_kernel(idx_ref, w_ref, x_ref, g_ref, b_ref, tab_hbm, o_ref, tab, stage0, stage1, sem):
    _load_table(tab_hbm, tab, sem)
    mask = _chunk_mask()

    def group(g, carry):
        base = pl.multiple_of(g * PEER_GROUP, PEER_GROUP)
        for tt in range(PEER_GROUP):
            t = base + tt
            rows = _gather_rows(idx_ref, t, tab, stage0 if tt % 2 == 0 else stage1)
            wrow = jnp.broadcast_to(w_ref[pl.ds(t, 1), :], mask.shape)
            ffn = _dot(jnp.where(mask, wrow, 0.0), rows)
            y = DEEPNORM_ALPHA * x_ref[t] + ffn
            mu = jnp.mean(jnp.mean(y, axis=1, keepdims=True), axis=0, keepdims=True)
            yc = y - mu
            var = jnp.mean(jnp.mean(yc * yc, axis=1, keepdims=True), axis=0, keepdims=True)
            o_ref[t] = yc * lax.rsqrt(var + LN_EPS) * g_ref[...] + b_ref[...]
        return carry

    lax.fori_loop(0, PEER_STEP_TOKENS // PEER_GROUP, group, 0)


def peer_experts(x, eidx_t, gates_t, u_tab, v_tab, g, b):
    t, d = x.shape
    TB, P = PEER_STEP_TOKENS, PEER_PICKS
    n = t // TB
    x3 = x.reshape(t, CHUNKS, LANES)
    gate_rep = jnp.repeat(gates_t, CHUNKS, axis=1)
    idx_spec = pl.BlockSpec((TB, P), lambda i: (i, 0), memory_space=pltpu.SMEM)
    x_spec = pl.BlockSpec((TB, CHUNKS, LANES), lambda i: (i, 0, 0))
    wide_spec = pl.BlockSpec((TB, P * CHUNKS), lambda i: (i, 0))
    chunk_row = pl.BlockSpec((CHUNKS, LANES), lambda i: (0, 0))
    hbm = pl.BlockSpec(memory_space=pl.ANY)
    scratch = [pltpu.VMEM(u_tab.shape, jnp.int32),
               pltpu.VMEM((P * WORD_ROWS, LANES), jnp.int32), pltpu.VMEM((P * WORD_ROWS, LANES), jnp.int32)]
    params = pltpu.CompilerParams(dimension_semantics=("arbitrary",), vmem_limit_bytes=PEER_VMEM_LIMIT)
    w = pl.pallas_call(
        _peer_act_kernel,
        grid=(n,),
        in_specs=[idx_spec, x_spec, wide_spec, hbm],
        out_specs=wide_spec,
        out_shape=jax.ShapeDtypeStruct((t, P * CHUNKS), F32),
        scratch_shapes=scratch + [pltpu.VMEM((PEER_GROUP, P * CHUNKS), F32), pltpu.SemaphoreType.DMA(())],
        name="peer_act",
        compiler_params=params,
    )(eidx_t, x3, gate_rep, u_tab)
    out = pl.pallas_call(
        _peer_mix_kernel,
        grid=(n,),
        in_specs=[idx_spec, wide_spec, x_spec, chunk_row, chunk_row, hbm],
        out_specs=x_spec,
        out_shape=jax.ShapeDtypeStruct((t, CHUNKS, LANES), F32),
        scratch_shapes=scratch + [pltpu.SemaphoreType.DMA(())],
        name="peer_mix",
        compiler_params=params,
    )(eidx_t, w, x3, g.reshape(CHUNKS, LANES), b.reshape(CHUNKS, LANES), v_tab)
    return out.reshape(t, d)


def _row(v):
    return v.astype(F32)[None, :]


def kernel(x, p, ssd_w_in, ssd_conv_w, ssd_conv_b, ssd_dt_bias, ssd_a_log, ssd_d, ssd_norm_g, ssd_w_out, moba_w_qkv, moba_w_out, conv_w_pw1, conv_b_pw1, conv_w_dw, conv_b_dw, conv_ln_g, conv_ln_b, conv_w_pw2, peer_w_q, peer_sub_keys, peer_u, peer_v, ln_mix_g, ln_mix_b, ln_ffn_g, ln_ffn_b, ple_w_gate, ple_w_proj):
    bsz, seq, d = x.shape
    t = bsz * seq
    x = x.reshape(t, d)
    zx_end = SSD_D_INNER + SSD_D_INNER + SSD_BC_DIM
    for i in range(DEPTH):
        kind, j = i % N_MIXERS, i // N_MIXERS
        g_mix, b_mix = _row(ln_mix_g[i]), _row(ln_mix_b[i])
        if kind == 0:
            w_in = ssd_w_in[j]
            w_dt = jnp.pad(w_in[:, zx_end:], ((0, 0), (0, LANES - SSD_N_HEADS)))
            zxbc = matmul(x, w_in[:, :zx_end].astype(BF16))
            dtp = matmul(x, w_dt.astype(BF16))
            y = ssd_core(zxbc, dtp, ssd_conv_w[j], ssd_conv_b[j], ssd_dt_bias[j], ssd_a_log[j], ssd_d[j],
                         ssd_norm_g[j], bsz, seq)
            x = matmul_ln(y, ssd_w_out[j].astype(BF16), x, g_mix, b_mix)
        elif kind == 1:
            qkv = matmul(x, moba_w_qkv[j].astype(BF16))
            att = moba_core(qkv, bsz, seq)
            x = matmul_ln(att, moba_w_out[j].astype(BF16), x, g_mix, b_mix)
        else:
            w1 = conv_w_pw1[j].astype(BF16)
            h = matmul_glu(x, w1[:, :D_MODEL], w1[:, D_MODEL:],
                           _row(conv_b_pw1[j][:D_MODEL]), _row(conv_b_pw1[j][D_MODEL:]))
            h = dwconv(h, conv_w_dw[j], conv_b_dw[j], bsz, seq)
            x = matmul_ln(h, conv_w_pw2[j].astype(BF16), x, g_mix, b_mix,
                          pre=(_row(conv_ln_g[j]), _row(conv_ln_b[j])))
        sk = peer_sub_keys[i].reshape(PEER_HEADS * 2, PEER_N_KEYS, PEER_HALF).astype(BF16)
        eidx, gates = peer_route(x, peer_w_q[i].astype(BF16), sk)
        x = peer_experts(x, eidx.T, gates.T, pack_expert_table(peer_u[i]), pack_expert_table(peer_v[i]),
                         ln_ffn_g[i].astype(F32), ln_ffn_b[i].astype(F32))
        x = ple_add(x, p[i].reshape(t, -1), ple_w_gate[i].astype(BF16), ple_w_proj[i].astype(BF16))
    return x.reshape(bsz, seq, d)
```

```python
import functools
import math

import jax
import jax.numpy as jnp
from jax import lax
from jax.experimental import pallas as pl
from jax.experimental.pallas import tpu as pltpu

F32 = jnp.float32
BF16 = jnp.bfloat16

D_MODEL = 1024
DEPTH = 4
N_MIXERS = 3

SSD_D_INNER = 2048
SSD_HEADDIM = 64
SSD_N_HEADS = 32
SSD_N_GROUPS = 4
SSD_HEADS_PER_GROUP = 8
SSD_D_STATE = 128
SSD_CONV_WIDTH = 4
SSD_CHUNK = 128
SSD_GROUP_LANES = SSD_D_INNER // SSD_N_GROUPS
SSD_BC_DIM = 2 * SSD_N_GROUPS * SSD_D_STATE

MOBA_HEAD_DIM = 64
MOBA_N_HEADS = 16
MOBA_BLOCK = 256
MOBA_TOPK = 3
MOBA_Q_BLOCK = 128

CONV_KERNEL = 31
CONV_PAD_ROWS = 32

PEER_N_KEYS = 128
PEER_HEADS = 8
PEER_TOPK = 16
PEER_QUERY_DIM = 256
PEER_HALF = 128
PEER_PICKS = PEER_HEADS * PEER_TOPK

LN_EPS = 1e-5
DEEPNORM_ALPHA = (2 * DEPTH) ** 0.25

LANES = 128
VMEM_LIMIT = 48 * 1024 * 1024

NEG_INF = float("-inf")


def _params(sem):
    return pltpu.CompilerParams(dimension_semantics=sem, vmem_limit_bytes=VMEM_LIMIT)


def _layer_norm(x, g, b):
    mu = jnp.mean(x, axis=-1, keepdims=True)
    xc = x - mu
    var = jnp.mean(xc * xc, axis=-1, keepdims=True)
    return xc * lax.rsqrt(var + LN_EPS) * g + b


def _sigmoid(x):
    return 1.0 / (1.0 + jnp.exp(-x))


def _silu(x):
    return x * _sigmoid(x)


def _dot(a, b):
    return jnp.dot(a.astype(BF16), b.astype(BF16), preferred_element_type=F32)


def _dot_nt(a, b):
    return lax.dot_general(a.astype(BF16), b.astype(BF16), (((1,), (1,)), ((), ())),
                           preferred_element_type=F32)


def _dot_exact(a, b):
    return jnp.dot(a, b, preferred_element_type=F32, precision=lax.Precision.HIGHEST)


def _mm_kernel(a_ref, w_ref, o_ref):
    o_ref[...] = _dot(a_ref[...], w_ref[...])


def matmul(a, w, tm=512, tn=512):
    m, k = a.shape
    n = w.shape[1]
    tm, tn = min(tm, m), min(tn, n)
    return pl.pallas_call(
        _mm_kernel,
        grid=(n // tn, m // tm),
        in_specs=[pl.BlockSpec((tm, k), lambda j, i: (i, 0)),
                  pl.BlockSpec((k, tn), lambda j, i: (0, j))],
        out_specs=pl.BlockSpec((tm, tn), lambda j, i: (i, j)),
        out_shape=jax.ShapeDtypeStruct((m, n), F32),
        name="mm",
        compiler_params=_params(("parallel", "parallel")),
    )(a, w)


def _mm_glu_kernel(a_ref, w1_ref, w2_ref, b1_ref, b2_ref, o_ref):
    a = a_ref[...].astype(BF16)
    h1 = jnp.dot(a, w1_ref[...], preferred_element_type=F32) + b1_ref[...]
    h2 = jnp.dot(a, w2_ref[...], preferred_element_type=F32) + b2_ref[...]
    o_ref[...] = h1 * _sigmoid(h2)


def matmul_glu(a, w1, w2, b1, b2, tm=512, tn=512):
    m, k = a.shape
    n = w1.shape[1]
    tm, tn = min(tm, m), min(tn, n)
    wspec = pl.BlockSpec((k, tn), lambda j, i: (0, j))
    bspec = pl.BlockSpec((1, tn), lambda j, i: (0, j))
    return pl.pallas_call(
        _mm_glu_kernel,
        grid=(n // tn, m // tm),
        in_specs=[pl.BlockSpec((tm, k), lambda j, i: (i, 0)), wspec, wspec, bspec, bspec],
        out_specs=pl.BlockSpec((tm, tn), lambda j, i: (i, j)),
        out_shape=jax.ShapeDtypeStruct((m, n), F32),
        name="mm_glu",
        compiler_params=_params(("parallel", "parallel")),
    )(a, w1, w2, b1, b2)


def _mm_ln_kernel(a_ref, w_ref, res_ref, g_ref, b_ref, o_ref):
    mix = _dot(a_ref[...], w_ref[...])
    o_ref[...] = _layer_norm(DEEPNORM_ALPHA * res_ref[...] + mix, g_ref[...], b_ref[...])


def _mm_lnsilu_ln_kernel(a_ref, pg_ref, pb_ref, w_ref, res_ref, g_ref, b_ref, o_ref):
    h = _silu(_layer_norm(a_ref[...], pg_ref[...], pb_ref[...]))
    mix = _dot(h, w_ref[...])
    o_ref[...] = _layer_norm(DEEPNORM_ALPHA * res_ref[...] + mix, g_ref[...], b_ref[...])


def matmul_ln(a, w, res, g, b, pre=None, tm=256):
    m, k = a.shape
    n = w.shape[1]
    tm = min(tm, m)
    row = lambda width: pl.BlockSpec((1, width), lambda i: (0, 0))
    a_spec = pl.BlockSpec((tm, k), lambda i: (i, 0))
    w_spec = pl.BlockSpec((k, n), lambda i: (0, 0))
    io_spec = pl.BlockSpec((tm, n), lambda i: (i, 0))
    if pre is None:
        body, ins, args = _mm_ln_kernel, [a_spec, w_spec, io_spec, row(n), row(n)], (a, w, res, g, b)
    else:
        body = _mm_lnsilu_ln_kernel
        ins = [a_spec, row(k), row(k), w_spec, io_spec, row(n), row(n)]
        args = (a, pre[0], pre[1], w, res, g, b)
    return pl.pallas_call(
        body, grid=(m // tm,), in_specs=ins, out_specs=io_spec,
        out_shape=jax.ShapeDtypeStruct((m, n), F32),
        name="mm_ln",
        compiler_params=_params(("parallel",)),
    )(*args)


def _ple_kernel(x_ref, p_ref, wg_ref, wp_ref, xres_ref, o_ref):
    gate = _sigmoid(_dot(x_ref[...], wg_ref[...]))
    o_ref[...] = xres_ref[...] + gate * _dot(p_ref[...], wp_ref[...])


def ple_add(x, p, wg, wp, tm=512, tn=512):
    m, k = x.shape
    kp = p.shape[1]
    n = wg.shape[1]
    tm, tn = min(tm, m), min(tn, n)
    return pl.pallas_call(
        _ple_kernel,
        grid=(n // tn, m // tm),
        in_specs=[pl.BlockSpec((tm, k), lambda j, i: (i, 0)),
                  pl.BlockSpec((tm, kp), lambda j, i: (i, 0)),
                  pl.BlockSpec((k, tn), lambda j, i: (0, j)),
                  pl.BlockSpec((kp, tn), lambda j, i: (0, j)),
                  pl.BlockSpec((tm, tn), lambda j, i: (i, j))],
        out_specs=pl.BlockSpec((tm, tn), lambda j, i: (i, j)),
        out_shape=jax.ShapeDtypeStruct((m, n), F32),
        name="ple_add",
        compiler_params=_params(("parallel", "parallel")),
    )(x, p, wg, wp, x)


def _ssd_kernel(z_ref, x_ref, bc_ref, dt_ref, cwx_ref, cwbc_ref, cbx_ref, cbbc_ref, dtb_ref, alog_ref,
                dexp_ref, ng_ref, expand_ref, o_ref, xpad, bcpad, state, yacc):
    L = SSD_CHUNK
    first = pl.program_id(1) == 0

    @pl.when(first)
    def _():
        xpad[0:8, :] = jnp.zeros((8, SSD_D_INNER), F32)
        bcpad[0:8, :] = jnp.zeros((8, SSD_BC_DIM), F32)
        state[...] = jnp.zeros_like(state)

    xpad[8:8 + L, :] = x_ref[...]
    bcpad[8:8 + L, :] = bc_ref[...]
    xs = cbx_ref[...]
    bc = cbbc_ref[...]
    for k in range(SSD_CONV_WIDTH):
        lo = 8 - (SSD_CONV_WIDTH - 1) + k
        xs = xs + cwx_ref[k:k + 1, :] * xpad[lo:lo + L, :]
        bc = bc + cwbc_ref[k:k + 1, :] * bcpad[lo:lo + L, :]
    xs = _silu(xs)
    bc = _silu(bc)
    xpad[0:8, :] = x_ref[L - 8:L, :]
    bcpad[0:8, :] = bc_ref[L - 8:L, :]

    lane = lax.broadcasted_iota(jnp.int32, (1, LANES), 1)
    dt_in = dt_ref[...] + dtb_ref[...]
    dt = jnp.maximum(dt_in, 0.0) + jnp.log1p(jnp.exp(-jnp.abs(dt_in)))
    dt = jnp.where(lane < SSD_N_HEADS, dt, 0.0)
    a = -jnp.exp(alog_ref[...])
    row = lax.broadcasted_iota(jnp.int32, (L, L), 0)
    col = lax.broadcasted_iota(jnp.int32, (L, L), 1)
    causal = col <= row
    a_cum = _dot_exact(causal.astype(F32), dt * a)
    a_cum_t = a_cum.T
    expand = expand_ref[...]
    dt_e = _dot_exact(dt, expand)
    acum_e = _dot_exact(a_cum, expand)
    alast_e = acum_e[L - 1:L, :]
    x_dt = xs * dt_e
    x_end = jnp.exp(alast_e - acum_e) * x_dt
    decay_in = jnp.exp(acum_e)

    lane_l = lax.broadcasted_iota(jnp.int32, (L, LANES), 1)
    for g in range(SSD_N_GROUPS):
        gl = slice(g * SSD_GROUP_LANES, (g + 1) * SSD_GROUP_LANES)
        b_g = bc[:, g * SSD_D_STATE:(g + 1) * SSD_D_STATE]
        c_g = bc[:, (SSD_N_GROUPS + g) * SSD_D_STATE:(SSD_N_GROUPS + g + 1) * SSD_D_STATE]
        cb = _dot_nt(c_g, b_g)
        prev = state[:, gl]
        yacc[:, gl] = _dot(c_g, prev) * decay_in[:, gl]
        new_states = _dot(b_g.T, x_end[:, gl])
        state[:, gl] = jnp.exp(alast_e[:, gl]) * prev + new_states
        for pr in range(SSD_HEADS_PER_GROUP // 2):
            ms = []
            for sub in range(2):
                h = g * SSD_HEADS_PER_GROUP + 2 * pr + sub
                seg = a_cum[:, h:h + 1] - a_cum_t[h:h + 1, :]
                ms.append(cb * jnp.exp(jnp.where(causal, seg, NEG_INF)))
            m_cat = jnp.concatenate(ms, axis=1)
            pl_ = slice(g * SSD_GROUP_LANES + pr * LANES, g * SSD_GROUP_LANES + (pr + 1) * LANES)
            xp = x_dt[:, pl_]
            x_bd = jnp.concatenate([jnp.where(lane_l < SSD_HEADDIM, xp, 0.0),
                                    jnp.where(lane_l >= SSD_HEADDIM, xp, 0.0)], axis=0)
            yacc[:, pl_] = yacc[:, pl_] + _dot(m_cat, x_bd)

    y = yacc[...] + dexp_ref[...] * xs
    gated = y * _silu(z_ref[...])
    for g in range(SSD_N_GROUPS):
        gl = slice(g * SSD_GROUP_LANES, (g + 1) * SSD_GROUP_LANES)
        gg = gated[:, gl]
        ms = jnp.mean(gg * gg, axis=-1, keepdims=True)
        o_ref[:, gl] = gg * lax.rsqrt(ms + LN_EPS) * ng_ref[:, gl]


def ssd_core(zxbc, dtp, conv_w, conv_b, dt_bias, a_log, d_skip, norm_g, bsz, seq):
    L = SSD_CHUNK
    nc = seq // L
    pad_h = LANES - SSD_N_HEADS
    head_of_lane = jnp.arange(SSD_D_INNER, dtype=jnp.int32) // SSD_HEADDIM
    expand = (jnp.arange(LANES, dtype=jnp.int32)[:, None] == head_of_lane[None, :]).astype(F32)
    dexp = jnp.repeat(d_skip.astype(F32), SSD_HEADDIM)[None, :]
    dtb = jnp.pad(dt_bias.astype(F32), (0, pad_h))[None, :]
    alog = jnp.pad(a_log.astype(F32), (0, pad_h))[None, :]
    cwx, cwbc = conv_w[:, :SSD_D_INNER], conv_w[:, SSD_D_INNER:]
    cbx, cbbc = conv_b[None, :SSD_D_INNER], conv_b[None, SSD_D_INNER:]
    chunk = lambda b, c: b * nc + c
    full = lambda shape: pl.BlockSpec(shape, lambda b, c: (0, 0))
    return pl.pallas_call(
        _ssd_kernel,
        grid=(bsz, nc),
        in_specs=[pl.BlockSpec((L, SSD_D_INNER), lambda b, c: (chunk(b, c), 0)),
                  pl.BlockSpec((L, SSD_D_INNER), lambda b, c: (chunk(b, c), 1)),
                  pl.BlockSpec((L, SSD_BC_DIM), lambda b, c: (chunk(b, c), 4)),
                  pl.BlockSpec((L, LANES), lambda b, c: (chunk(b, c), 0)),
                  full((SSD_CONV_WIDTH, SSD_D_INNER)), full((SSD_CONV_WIDTH, SSD_BC_DIM)),
                  full((1, SSD_D_INNER)), full((1, SSD_BC_DIM)),
                  full((1, LANES)), full((1, LANES)),
                  full((1, SSD_D_INNER)), full((1, SSD_D_INNER)),
                  full((LANES, SSD_D_INNER))],
        out_specs=pl.BlockSpec((L, SSD_D_INNER), lambda b, c: (chunk(b, c), 0)),
        out_shape=jax.ShapeDtypeStruct((bsz * seq, SSD_D_INNER), F32),
        scratch_shapes=[pltpu.VMEM((L + 8, SSD_D_INNER), F32),
                        pltpu.VMEM((L + 8, SSD_BC_DIM), F32),
                        pltpu.VMEM((SSD_D_STATE, SSD_D_INNER), F32),
                        pltpu.VMEM((L, SSD_D_INNER), F32)],
        name="ssd_core",
        compiler_params=_params(("parallel", "arbitrary")),
    )(zxbc, zxbc, zxbc, dtp, cwx, cwbc, cbx, cbbc, dtb, alog, dexp, norm_g[None, :], expand)


def _moba_kernel(slopes_ref, qt_ref, k_ref, vt_ref, o_ref, kmean, sel_buf, logit_buf, max_buf, sum_buf):
    hp = pl.program_id(1)
    qi = pl.program_id(2)
    Q, KB, DH = MOBA_Q_BLOCK, MOBA_BLOCK, MOBA_HEAD_DIM
    n_blk = k_ref.shape[0] // KB
    own = (qi * Q) // KB
    q0 = qi * Q
    scale = DH ** -0.5

    @pl.when(qi == 0)
    def _():
        for n in range(n_blk):
            kmean[n:n + 1, :] = jnp.mean(k_ref[n * KB:(n + 1) * KB, :], axis=0, keepdims=True)

    qt = qt_ref[0]
    feat = lax.broadcasted_iota(jnp.int32, (2 * DH, Q), 0)
    q_bd = jnp.concatenate([jnp.where(feat < DH, qt, 0.0), jnp.where(feat >= DH, qt, 0.0)], axis=1)
    q_bd = q_bd.astype(BF16)
    blk = lax.broadcasted_iota(jnp.int32, (n_blk, 2 * Q), 0)
    gate = jnp.where(blk < own, _dot(kmean[...], q_bd), NEG_INF)
    rank = jnp.zeros((n_blk, 2 * Q), jnp.int32)
    for m in range(n_blk):
        gm = gate[m:m + 1, :]
        beats = (gm > gate) | ((gm == gate) & (blk > m))
        rank = rank + beats.astype(jnp.int32)
    sel_buf[...] = ((rank < MOBA_TOPK) & (blk < own)).astype(F32)
    max_buf[...] = jnp.full(max_buf.shape, NEG_INF, F32)
    sum_buf[...] = jnp.zeros(sum_buf.shape, F32)
    slope = jnp.concatenate([jnp.full((1, Q), slopes_ref[2 * hp], F32),
                             jnp.full((1, Q), slopes_ref[2 * hp + 1], F32)], axis=1)
    qpos = lax.broadcasted_iota(jnp.int32, (KB, 2 * Q), 1) & (Q - 1)
    qk = qpos - lax.broadcasted_iota(jnp.int32, (KB, 2 * Q), 0)

    def fold(x):
        return x.reshape(KB // 8, 8, 2 * Q)

    def pass1(n, carry):
        start = pl.multiple_of(n * KB, KB)
        dist = qk + (q0 - n * KB)
        keep = (sel_buf[pl.ds(n, 1), :] > 0.0) | ((dist >= 0) & (n == own))
        lg = _dot(k_ref[pl.ds(start, KB), :], q_bd) * scale - slope * dist.astype(F32)
        lg = jnp.where(keep, lg, NEG_INF)
        logit_buf[n] = lg
        max_buf[...] = jnp.maximum(max_buf[...], jnp.max(fold(lg), axis=0))
        return carry

    lax.fori_loop(0, own + 1, pass1, 0)
    col_max = jnp.max(max_buf[...], axis=0, keepdims=True)

    def pass2(n, accs):
        p = jnp.exp(logit_buf[n] - col_max)
        sum_buf[...] = sum_buf[...] + jnp.sum(fold(p), axis=0)
        vt = vt_ref[0, n]
        return tuple(accs[i] + _dot(vt[i * DH:(i + 1) * DH, :], p[:, i * Q:(i + 1) * Q]) for i in range(2))

    accs = lax.fori_loop(0, own + 1, pass2, (jnp.zeros((DH, Q), F32), jnp.zeros((DH, Q), F32)))
    denom = jnp.sum(sum_buf[...], axis=0, keepdims=True)
    for i in range(2):
        o_ref[0, i * DH:(i + 1) * DH, :] = accs[i] / denom[:, i * Q:(i + 1) * Q]


def moba_core(qkv, bsz, seq):
    Q, KB = MOBA_Q_BLOCK, MOBA_BLOCK
    nq, n_blk = seq // Q, seq // KB
    n_pairs = MOBA_N_HEADS // 2
    slopes = 2.0 ** (-8.0 * jnp.arange(1, MOBA_N_HEADS + 1, dtype=F32) / MOBA_N_HEADS)
    q_t = qkv[:, :D_MODEL].reshape(bsz, seq, D_MODEL).transpose(0, 2, 1)
    v_t = qkv[:, 2 * D_MODEL:].reshape(bsz, n_blk, KB, D_MODEL).transpose(0, 1, 3, 2)
    grid_spec = pltpu.PrefetchScalarGridSpec(
        num_scalar_prefetch=1,
        grid=(bsz, n_pairs, nq),
        in_specs=[pl.BlockSpec((1, LANES, Q), lambda b, h, i, s: (b, h, i)),
                  pl.BlockSpec((seq, LANES), lambda b, h, i, s: (b, n_pairs + h)),
                  pl.BlockSpec((1, n_blk, LANES, KB), lambda b, h, i, s: (b, 0, h, 0))],
        out_specs=pl.BlockSpec((1, LANES, Q), lambda b, h, i, s: (b, h, i)),
        scratch_shapes=[pltpu.VMEM((n_blk, LANES), F32),
                        pltpu.VMEM((n_blk, 2 * Q), F32),
                        pltpu.VMEM((n_blk, KB, 2 * Q), F32),
                        pltpu.VMEM((8, 2 * Q), F32), pltpu.VMEM((8, 2 * Q), F32)],
    )
    out_t = pl.pallas_call(
        _moba_kernel, grid_spec=grid_spec,
        out_shape=jax.ShapeDtypeStruct((bsz, D_MODEL, seq), F32),
        name="moba_core",
        compiler_params=_params(("parallel", "parallel", "arbitrary")),
    )(slopes, q_t, qkv, v_t)
    return out_t.transpose(0, 2, 1).reshape(bsz * seq, D_MODEL)


CONV_ROWS = 256


def _dwconv_kernel(h_ref, w_ref, b_ref, o_ref, pad):
    seq = h_ref.shape[1]
    pad[0:CONV_PAD_ROWS, :] = jnp.zeros((CONV_PAD_ROWS, pad.shape[1]), F32)
    pad[CONV_PAD_ROWS:CONV_PAD_ROWS + seq, :] = h_ref[0]
    first = CONV_PAD_ROWS - (CONV_KERNEL - 1)
    for c in range(seq // CONV_ROWS):
        r0 = c * CONV_ROWS
        acc = jnp.broadcast_to(b_ref[...], (CONV_ROWS, pad.shape[1]))
        for k in range(CONV_KERNEL):
            acc = acc + w_ref[k:k + 1, :] * pad[r0 + first + k:r0 + first + k + CONV_ROWS, :]
        o_ref[0, r0:r0 + CONV_ROWS, :] = acc


def dwconv(h, w, b, bsz, seq, tc=256):
    chans = h.shape[1]
    h3 = h.reshape(bsz, seq, chans)
    out = pl.pallas_call(
        _dwconv_kernel,
        grid=(bsz, chans // tc),
        in_specs=[pl.BlockSpec((1, seq, tc), lambda b, c: (b, 0, c)),
                  pl.BlockSpec((CONV_KERNEL, tc), lambda b, c: (0, c)),
                  pl.BlockSpec((1, tc), lambda b, c: (0, c))],
        out_specs=pl.BlockSpec((1, seq, tc), lambda b, c: (b, 0, c)),
        out_shape=jax.ShapeDtypeStruct((bsz, seq, chans), F32),
        scratch_shapes=[pltpu.VMEM((CONV_PAD_ROWS + seq, tc), F32)],
        name="dwconv",
        compiler_params=_params(("parallel", "parallel")),
    )(h3, w, b[None, :])
    return out.reshape(bsz * seq, chans)


PEER_CAND_PER_A = tuple(PEER_TOPK // (a + 1) for a in range(PEER_TOPK))
PEER_CAND_ROWS = 56


def _top16_rows(cur, dst_val, dst_idx, payload=None):
    rows = cur.shape[0]
    riota = lax.broadcasted_iota(jnp.int32, cur.shape, 0).astype(F32)
    for i in range(PEER_TOPK):
        m = jnp.max(cur, axis=0, keepdims=True)
        am = jnp.min(jnp.where(cur == m, riota, float(rows)), axis=0, keepdims=True)
        hit = riota == am
        dst_val[i:i + 1, :] = m
        if payload is None:
            dst_idx[i:i + 1, :] = am
        else:
            dst_idx[i:i + 1, :] = jnp.max(jnp.where(hit, payload, -1.0), axis=0, keepdims=True)
        cur = jnp.where(hit, NEG_INF, cur)


def _peer_route_kernel(x_ref, wq_ref, sk_ref, eidx_ref, gate_ref, v1, i1, v2, i2, cand, cidx, best, bidx):
    K = PEER_TOPK
    q = _dot(x_ref[...], wq_ref[...])
    cand[PEER_CAND_ROWS - 8:PEER_CAND_ROWS, :] = jnp.full((8, cand.shape[1]), NEG_INF, F32)
    cidx[PEER_CAND_ROWS - 8:PEER_CAND_ROWS, :] = jnp.zeros((8, cand.shape[1]), F32)
    for h in range(PEER_HEADS):
        for c, (vv, ii) in enumerate(((v1, i1), (v2, i2))):
            hc = 2 * h + c
            s_t = _dot_nt(sk_ref[hc], q[:, hc * PEER_HALF:(hc + 1) * PEER_HALF])
            _top16_rows(s_t, vv, ii)
        off = 0
        for a, n_b in enumerate(PEER_CAND_PER_A):
            cand[off:off + n_b, :] = v1[a:a + 1, :] + v2[0:n_b, :]
            cidx[off:off + n_b, :] = i1[a:a + 1, :] * float(PEER_N_KEYS) + i2[0:n_b, :]
            off += n_b
        _top16_rows(cand[...], best, bidx, payload=cidx[...])
        eidx_ref[h * K:(h + 1) * K, :] = bidx[...].astype(jnp.int32)
        b = best[...]
        e = jnp.exp(b - b[0:1, :])
        gate_ref[h * K:(h + 1) * K, :] = e / jnp.sum(e, axis=0, keepdims=True)


def peer_route(x, wq, sk, tm=256):
    t, d = x.shape
    tm = min(tm, t)
    K = PEER_TOPK
    out_spec = pl.BlockSpec((PEER_PICKS, tm), lambda i: (0, i))
    return pl.pallas_call(
        _peer_route_kernel,
        grid=(t // tm,),
        in_specs=[pl.BlockSpec((tm, d), lambda i: (i, 0)),
                  pl.BlockSpec(wq.shape, lambda i: (0, 0)),
                  pl.BlockSpec(sk.shape, lambda i: (0, 0, 0))],
        out_specs=[out_spec, out_spec],
        out_shape=[jax.ShapeDtypeStruct((PEER_PICKS, t), jnp.int32),
                   jax.ShapeDtypeStruct((PEER_PICKS, t), F32)],
        scratch_shapes=[pltpu.VMEM((K, tm), F32)] * 4
                       + [pltpu.VMEM((PEER_CAND_ROWS, tm), F32)] * 2
                       + [pltpu.VMEM((K, tm), F32)] * 2,
        name="peer_route",
        compiler_params=_params(("parallel",)),
    )(x, wq, sk)


PEER_STEP_TOKENS = 64
PEER_GROUP = 8
CHUNKS = D_MODEL // LANES
WORD_ROWS = CHUNKS // 2
PEER_VMEM_LIMIT = 44 * 1024 * 1024


def _gelu_tanh(x):
    return 0.5 * x * (1.0 + jnp.tanh(math.sqrt(2.0 / math.pi) * (x + 0.044715 * (x * x * x))))


def pack_expert_table(w):
    n = w.shape[0]
    bits = lax.bitcast_convert_type(w.astype(jnp.bfloat16), jnp.uint16).astype(jnp.uint32)
    bits = bits.reshape(n, WORD_ROWS, 2, LANES)
    words = lax.bitcast_convert_type(bits[:, :, 0, :] | (bits[:, :, 1, :] << 16), jnp.int32)
    return words.reshape(n * WORD_ROWS, LANES)


def _load_table(tab_hbm, tab, sem):
    @pl.when(pl.program_id(0) == 0)
    def _():
        cp = pltpu.make_async_copy(tab_hbm, tab, sem)
        cp.start()
        cp.wait()


def _gather_rows(idx_ref, t, tab, stage):
    for j in range(PEER_PICKS):
        stage[j * WORD_ROWS:(j + 1) * WORD_ROWS, :] = tab[pl.ds(pl.multiple_of(idx_ref[t, j], WORD_ROWS), WORD_ROWS), :]


def _chunk_mask():
    sub = lax.broadcasted_iota(jnp.int32, (CHUNKS, PEER_PICKS * CHUNKS), 0)
    lane = lax.broadcasted_iota(jnp.int32, (CHUNKS, PEER_PICKS * CHUNKS), 1)
    return sub == (lane & (CHUNKS - 1))


def _for_each_token(idx_ref, tab, stages, finish):
    _gather_rows(idx_ref, 0, tab, stages[0])

    def group(g, carry):
        base = pl.multiple_of(g * PEER_GROUP, PEER_GROUP)
        for tt in range(PEER_GROUP):
            t = base + tt
            _gather_rows(idx_ref, jnp.minimum(t + 1, PEER_STEP_TOKENS - 1), tab, stages[(tt + 1) % 2])
            finish(t, pltpu.bitcast(stages[tt % 2][...], jnp.bfloat16))
        return carry

    lax.fori_loop(0, PEER_STEP_TOKENS // PEER_GROUP, group, 0)


def _peer_act_kernel(idx_ref, x_ref, gate_ref, tab_hbm, w_ref, tab, stage0, stage1, acts, sem):
    _load_table(tab_hbm, tab, sem)
    mask = _chunk_mask()

    def finish(t, rows):
        r = _dot_nt(x_ref[t], rows)
        acts[pl.ds(t, 1), :] = jnp.sum(jnp.where(mask, r, 0.0), axis=0, keepdims=True)

    _for_each_token(idx_ref, tab, (stage0, stage1), finish)
    r_id = lax.broadcasted_iota(jnp.int32, (LANES, LANES), 0) // CHUNKS
    c_id = lax.broadcasted_iota(jnp.int32, (LANES, LANES), 1) // CHUNKS
    group_ones = (r_id == c_id).astype(F32)
    for c in range(CHUNKS):
        sl = (slice(None), slice(c * LANES, (c + 1) * LANES))
        w_ref[sl] = gate_ref[sl] * _gelu_tanh(_dot_exact(acts[sl], group_ones))


def _peer_mix_kernel(idx_ref, w_ref, x_ref, g_ref, b_ref, tab_hbm, o_ref, tab, stage0, stage1, sem):
    _load_table(tab_hbm, tab, sem)
    mask = _chunk_mask()

    def finish(t, rows):
        wrow = jnp.broadcast_to(w_ref[pl.ds(t, 1), :], mask.shape)
        o_ref[t] = DEEPNORM_ALPHA * x_ref[t] + _dot(jnp.where(mask, wrow, 0.0), rows)

    _for_each_token(idx_ref, tab, (stage0, stage1), finish)
    y = o_ref[...]
    mu = jnp.mean(jnp.mean(y, axis=2, keepdims=True), axis=1, keepdims=True)
    yc = y - mu
    var = jnp.mean(jnp.mean(yc * yc, axis=2, keepdims=True), axis=1, keepdims=True)
    o_ref[...] = yc * lax.rsqrt(var + LN_EPS) * g_ref[...][None] + b_ref[...][None]


def peer_experts(x, eidx_t, gates_t, u_tab, v_tab, g, b):
    t, d = x.shape
    TB, P = PEER_STEP_TOKENS, PEER_PICKS
    n = t // TB
    x3 = x.reshape(t, CHUNKS, LANES)
    gate_rep = jnp.repeat(gates_t, CHUNKS, axis=1)
    idx_spec = pl.BlockSpec((TB, P), lambda i: (i, 0), memory_space=pltpu.SMEM)
    x_spec = pl.BlockSpec((TB, CHUNKS, LANES), lambda i: (i, 0, 0))
    wide_spec = pl.BlockSpec((TB, P * CHUNKS), lambda i: (i, 0))
    chunk_row = pl.BlockSpec((CHUNKS, LANES), lambda i: (0, 0))
    hbm = pl.BlockSpec(memory_space=pl.ANY)
    scratch = [pltpu.VMEM(u_tab.shape, jnp.int32),
               pltpu.VMEM((P * WORD_ROWS, LANES), jnp.int32), pltpu.VMEM((P * WORD_ROWS, LANES), jnp.int32)]
    params = pltpu.CompilerParams(dimension_semantics=("arbitrary",), vmem_limit_bytes=PEER_VMEM_LIMIT)
    w = pl.pallas_call(
        _peer_act_kernel,
        grid=(n,),
        in_specs=[idx_spec, x_spec, wide_spec, hbm],
        out_specs=wide_spec,
        out_shape=jax.ShapeDtypeStruct((t, P * CHUNKS), F32),
        scratch_shapes=scratch + [pltpu.VMEM((TB, P * CHUNKS), F32), pltpu.SemaphoreType.DMA(())],
        name="peer_act",
        compiler_params=params,
    )(eidx_t, x3, gate_rep, u_tab)
    out = pl.pallas_call(
        _peer_mix_kernel,
        grid=(n,),
        in_specs=[idx_spec, wide_spec, x_spec, chunk_row, chunk_row, hbm],
        out_specs=x_spec,
        out_shape=jax.ShapeDtypeStruct((t, CHUNKS, LANES), F32),
        scratch_shapes=scratch + [pltpu.SemaphoreType.DMA(())],
        name="peer_mix",
        compiler_params=params,
    )(eidx_t, w, x3, g.reshape(CHUNKS, LANES), b.reshape(CHUNKS, LANES), v_tab)
    return out.reshape(t, d)


def _row(v):
    return v.astype(F32)[None, :]


def kernel(x, p, ssd_w_in, ssd_conv_w, ssd_conv_b, ssd_dt_bias, ssd_a_log, ssd_d, ssd_norm_g, ssd_w_out, moba_w_qkv, moba_w_out, conv_w_pw1, conv_b_pw1, conv_w_dw, conv_b_dw, conv_ln_g, conv_ln_b, conv_w_pw2, peer_w_q, peer_sub_keys, peer_u, peer_v, ln_mix_g, ln_mix_b, ln_ffn_g, ln_ffn_b, ple_w_gate, ple_w_proj):
    bsz, seq, d = x.shape
    t = bsz * seq
    x = x.reshape(t, d)
    zx_end = SSD_D_INNER + SSD_D_INNER + SSD_BC_DIM
    for i in range(DEPTH):
        kind, j = i % N_MIXERS, i // N_MIXERS
        g_mix, b_mix = _row(ln_mix_g[i]), _row(ln_mix_b[i])
        if kind == 0:
            w_in = ssd_w_in[j]
            w_dt = jnp.pad(w_in[:, zx_end:], ((0, 0), (0, LANES - SSD_N_HEADS)))
            zxbc = matmul(x, w_in[:, :zx_end].astype(BF16))
            dtp = matmul(x, w_dt.astype(BF16))
            y = ssd_core(zxbc, dtp, ssd_conv_w[j], ssd_conv_b[j], ssd_dt_bias[j], ssd_a_log[j], ssd_d[j],
                         ssd_norm_g[j], bsz, seq)
            x = matmul_ln(y, ssd_w_out[j].astype(BF16), x, g_mix, b_mix)
        elif kind == 1:
            qkv = matmul(x, moba_w_qkv[j].astype(BF16))
            att = moba_core(qkv, bsz, seq)
            x = matmul_ln(att, moba_w_out[j].astype(BF16), x, g_mix, b_mix)
        else:
            w1 = conv_w_pw1[j].astype(BF16)
            h = matmul_glu(x, w1[:, :D_MODEL], w1[:, D_MODEL:],
                           _row(conv_b_pw1[j][:D_MODEL]), _row(conv_b_pw1[j][D_MODEL:]))
            h = dwconv(h, conv_w_dw[j], conv_b_dw[j], bsz, seq)
            x = matmul_ln(h, conv_w_pw2[j].astype(BF16), x, g_mix, b_mix,
                          pre=(_row(conv_ln_g[j]), _row(conv_ln_b[j])))
        sk = peer_sub_keys[i].reshape(PEER_HEADS * 2, PEER_N_KEYS, PEER_HALF).astype(BF16)
        eidx, gates = peer_route(x, peer_w_q[i].astype(BF16), sk)
        x = peer_experts(x, eidx.T * WORD_ROWS, gates.T, pack_expert_table(peer_u[i]), pack_expert_table(peer_v[i]),
                         ln_ffn_g[i].astype(F32), ln_ffn_b[i].astype(F32))
        x = ple_add(x, p[i].reshape(t, -1), ple_w_gate[i].astype(BF16), ple_w_proj[i].astype(BF16))
    return x.reshape(bsz, seq, d)
```

```python
import functools
import math

import jax
import jax.numpy as jnp
from jax import lax
from jax.experimental import pallas as pl
from jax.experimental.pallas import tpu as pltpu

F32 = jnp.float32
BF16 = jnp.bfloat16

D_MODEL = 1024
DEPTH = 4
N_MIXERS = 3

SSD_D_INNER = 2048
SSD_HEADDIM = 64
SSD_N_HEADS = 32
SSD_N_GROUPS = 4
SSD_HEADS_PER_GROUP = 8
SSD_D_STATE = 128
SSD_CONV_WIDTH = 4
SSD_CHUNK = 128
SSD_GROUP_LANES = SSD_D_INNER // SSD_N_GROUPS
SSD_BC_DIM = 2 * SSD_N_GROUPS * SSD_D_STATE

MOBA_HEAD_DIM = 64
MOBA_N_HEADS = 16
MOBA_BLOCK = 256
MOBA_TOPK = 3
MOBA_Q_BLOCK = 128

CONV_KERNEL = 31
CONV_PAD_ROWS = 32

PEER_N_KEYS = 128
PEER_HEADS = 8
PEER_TOPK = 16
PEER_QUERY_DIM = 256
PEER_HALF = 128
PEER_PICKS = PEER_HEADS * PEER_TOPK

LN_EPS = 1e-5
DEEPNORM_ALPHA = (2 * DEPTH) ** 0.25

LANES = 128
VMEM_LIMIT = 48 * 1024 * 1024

NEG_INF = float("-inf")


def _params(sem):
    return pltpu.CompilerParams(dimension_semantics=sem, vmem_limit_bytes=VMEM_LIMIT)


def _layer_norm(x, g, b):
    mu = jnp.mean(x, axis=-1, keepdims=True)
    xc = x - mu
    var = jnp.mean(xc * xc, axis=-1, keepdims=True)
    return xc * lax.rsqrt(var + LN_EPS) * g + b


def _sigmoid(x):
    return 1.0 / (1.0 + jnp.exp(-x))


def _silu(x):
    return x * _sigmoid(x)


def _dot(a, b):
    return jnp.dot(a.astype(BF16), b.astype(BF16), preferred_element_type=F32)


def _dot_nt(a, b):
    return lax.dot_general(a.astype(BF16), b.astype(BF16), (((1,), (1,)), ((), ())),
                           preferred_element_type=F32)


def _dot_exact(a, b):
    return jnp.dot(a, b, preferred_element_type=F32, precision=lax.Precision.HIGHEST)


def _mm_kernel(a_ref, w_ref, o_ref):
    o_ref[...] = _dot(a_ref[...], w_ref[...])


def matmul(a, w, tm=512, tn=512):
    m, k = a.shape
    n = w.shape[1]
    tm, tn = min(tm, m), min(tn, n)
    return pl.pallas_call(
        _mm_kernel,
        grid=(n // tn, m // tm),
        in_specs=[pl.BlockSpec((tm, k), lambda j, i: (i, 0)),
                  pl.BlockSpec((k, tn), lambda j, i: (0, j))],
        out_specs=pl.BlockSpec((tm, tn), lambda j, i: (i, j)),
        out_shape=jax.ShapeDtypeStruct((m, n), F32),
        name="mm",
        compiler_params=_params(("parallel", "parallel")),
    )(a, w)


def _mm_glu_kernel(a_ref, w1_ref, w2_ref, b1_ref, b2_ref, o_ref):
    a = a_ref[...].astype(BF16)
    h1 = jnp.dot(a, w1_ref[...], preferred_element_type=F32) + b1_ref[...]
    h2 = jnp.dot(a, w2_ref[...], preferred_element_type=F32) + b2_ref[...]
    o_ref[...] = h1 * _sigmoid(h2)


def matmul_glu(a, w1, w2, b1, b2, tm=512, tn=512):
    m, k = a.shape
    n = w1.shape[1]
    tm, tn = min(tm, m), min(tn, n)
    wspec = pl.BlockSpec((k, tn), lambda j, i: (0, j))
    bspec = pl.BlockSpec((1, tn), lambda j, i: (0, j))
    return pl.pallas_call(
        _mm_glu_kernel,
        grid=(n // tn, m // tm),
        in_specs=[pl.BlockSpec((tm, k), lambda j, i: (i, 0)), wspec, wspec, bspec, bspec],
        out_specs=pl.BlockSpec((tm, tn), lambda j, i: (i, j)),
        out_shape=jax.ShapeDtypeStruct((m, n), F32),
        name="mm_glu",
        compiler_params=_params(("parallel", "parallel")),
    )(a, w1, w2, b1, b2)


def _mm_ln_kernel(a_ref, w_ref, res_ref, g_ref, b_ref, o_ref):
    mix = _dot(a_ref[...], w_ref[...])
    o_ref[...] = _layer_norm(DEEPNORM_ALPHA * res_ref[...] + mix, g_ref[...], b_ref[...])


def _mm_lnsilu_ln_kernel(a_ref, pg_ref, pb_ref, w_ref, res_ref, g_ref, b_ref, o_ref):
    h = _silu(_layer_norm(a_ref[...], pg_ref[...], pb_ref[...]))
    mix = _dot(h, w_ref[...])
    o_ref[...] = _layer_norm(DEEPNORM_ALPHA * res_ref[...] + mix, g_ref[...], b_ref[...])


def matmul_ln(a, w, res, g, b, pre=None, tm=256):
    m, k = a.shape
    n = w.shape[1]
    tm = min(tm, m)
    row = lambda width: pl.BlockSpec((1, width), lambda i: (0, 0))
    a_spec = pl.BlockSpec((tm, k), lambda i: (i, 0))
    w_spec = pl.BlockSpec((k, n), lambda i: (0, 0))
    io_spec = pl.BlockSpec((tm, n), lambda i: (i, 0))
    if pre is None:
        body, ins, args = _mm_ln_kernel, [a_spec, w_spec, io_spec, row(n), row(n)], (a, w, res, g, b)
    else:
        body = _mm_lnsilu_ln_kernel
        ins = [a_spec, row(k), row(k), w_spec, io_spec, row(n), row(n)]
        args = (a, pre[0], pre[1], w, res, g, b)
    return pl.pallas_call(
        body, grid=(m // tm,), in_specs=ins, out_specs=io_spec,
        out_shape=jax.ShapeDtypeStruct((m, n), F32),
        name="mm_ln",
        compiler_params=_params(("parallel",)),
    )(*args)


def _ple_kernel(x_ref, p_ref, wg_ref, wp_ref, xres_ref, o_ref):
    gate = _sigmoid(_dot(x_ref[...], wg_ref[...]))
    o_ref[...] = xres_ref[...] + gate * _dot(p_ref[...], wp_ref[...])


def ple_add(x, p, wg, wp, tm=512, tn=512):
    m, k = x.shape
    kp = p.shape[1]
    n = wg.shape[1]
    tm, tn = min(tm, m), min(tn, n)
    return pl.pallas_call(
        _ple_kernel,
        grid=(n // tn, m // tm),
        in_specs=[pl.BlockSpec((tm, k), lambda j, i: (i, 0)),
                  pl.BlockSpec((tm, kp), lambda j, i: (i, 0)),
                  pl.BlockSpec((k, tn), lambda j, i: (0, j)),
                  pl.BlockSpec((kp, tn), lambda j, i: (0, j)),
                  pl.BlockSpec((tm, tn), lambda j, i: (i, j))],
        out_specs=pl.BlockSpec((tm, tn), lambda j, i: (i, j)),
        out_shape=jax.ShapeDtypeStruct((m, n), F32),
        name="ple_add",
        compiler_params=_params(("parallel", "parallel")),
    )(x, p, wg, wp, x)


def _ssd_kernel(z_ref, x_ref, bc_ref, dt_ref, cwx_ref, cwbc_ref, cbx_ref, cbbc_ref, dtb_ref, alog_ref,
                dexp_ref, ng_ref, expand_ref, o_ref, xpad, bcpad, state, yacc):
    L = SSD_CHUNK
    first = pl.program_id(1) == 0

    @pl.when(first)
    def _():
        xpad[0:8, :] = jnp.zeros((8, SSD_D_INNER), F32)
        bcpad[0:8, :] = jnp.zeros((8, SSD_BC_DIM), F32)
        state[...] = jnp.zeros_like(state)

    xpad[8:8 + L, :] = x_ref[...]
    bcpad[8:8 + L, :] = bc_ref[...]
    xs = cbx_ref[...]
    bc = cbbc_ref[...]
    for k in range(SSD_CONV_WIDTH):
        lo = 8 - (SSD_CONV_WIDTH - 1) + k
        xs = xs + cwx_ref[k:k + 1, :] * xpad[lo:lo + L, :]
        bc = bc + cwbc_ref[k:k + 1, :] * bcpad[lo:lo + L, :]
    xs = _silu(xs)
    bc = _silu(bc)
    xpad[0:8, :] = x_ref[L - 8:L, :]
    bcpad[0:8, :] = bc_ref[L - 8:L, :]

    lane = lax.broadcasted_iota(jnp.int32, (1, LANES), 1)
    dt_in = dt_ref[...] + dtb_ref[...]
    dt = jnp.maximum(dt_in, 0.0) + jnp.log1p(jnp.exp(-jnp.abs(dt_in)))
    dt = jnp.where(lane < SSD_N_HEADS, dt, 0.0)
    a = -jnp.exp(alog_ref[...])
    row = lax.broadcasted_iota(jnp.int32, (L, L), 0)
    col = lax.broadcasted_iota(jnp.int32, (L, L), 1)
    causal = col <= row
    a_cum = _dot_exact(causal.astype(F32), dt * a)
    a_cum_t = a_cum.T
    expand = expand_ref[...]
    dt_e = _dot_exact(dt, expand)
    acum_e = _dot_exact(a_cum, expand)
    alast_e = acum_e[L - 1:L, :]
    x_dt = xs * dt_e
    x_end = jnp.exp(alast_e - acum_e) * x_dt
    decay_in = jnp.exp(acum_e)

    lane_l = lax.broadcasted_iota(jnp.int32, (L, LANES), 1)
    for g in range(SSD_N_GROUPS):
        gl = slice(g * SSD_GROUP_LANES, (g + 1) * SSD_GROUP_LANES)
        b_g = bc[:, g * SSD_D_STATE:(g + 1) * SSD_D_STATE]
        c_g = bc[:, (SSD_N_GROUPS + g) * SSD_D_STATE:(SSD_N_GROUPS + g + 1) * SSD_D_STATE]
        cb = _dot_nt(c_g, b_g)
        prev = state[:, gl]
        yacc[:, gl] = _dot(c_g, prev) * decay_in[:, gl]
        new_states = _dot(b_g.T, x_end[:, gl])
        state[:, gl] = jnp.exp(alast_e[:, gl]) * prev + new_states
        for pr in range(SSD_HEADS_PER_GROUP // 2):
            ms = []
            for sub in range(2):
                h = g * SSD_HEADS_PER_GROUP + 2 * pr + sub
                seg = a_cum[:, h:h + 1] - a_cum_t[h:h + 1, :]
                ms.append(cb * jnp.exp(jnp.where(causal, seg, NEG_INF)))
            m_cat = jnp.concatenate(ms, axis=1)
            pl_ = slice(g * SSD_GROUP_LANES + pr * LANES, g * SSD_GROUP_LANES + (pr + 1) * LANES)
            xp = x_dt[:, pl_]
            x_bd = jnp.concatenate([jnp.where(lane_l < SSD_HEADDIM, xp, 0.0),
                                    jnp.where(lane_l >= SSD_HEADDIM, xp, 0.0)], axis=0)
            yacc[:, pl_] = yacc[:, pl_] + _dot(m_cat, x_bd)

    y = yacc[...] + dexp_ref[...] * xs
    gated = y * _silu(z_ref[...])
    for g in range(SSD_N_GROUPS):
        gl = slice(g * SSD_GROUP_LANES, (g + 1) * SSD_GROUP_LANES)
        gg = gated[:, gl]
        ms = jnp.mean(gg * gg, axis=-1, keepdims=True)
        o_ref[:, gl] = gg * lax.rsqrt(ms + LN_EPS) * ng_ref[:, gl]


def ssd_core(zxbc, dtp, conv_w, conv_b, dt_bias, a_log, d_skip, norm_g, bsz, seq):
    L = SSD_CHUNK
    nc = seq // L
    pad_h = LANES - SSD_N_HEADS
    head_of_lane = jnp.arange(SSD_D_INNER, dtype=jnp.int32) // SSD_HEADDIM
    expand = (jnp.arange(LANES, dtype=jnp.int32)[:, None] == head_of_lane[None, :]).astype(F32)
    dexp = jnp.repeat(d_skip.astype(F32), SSD_HEADDIM)[None, :]
    dtb = jnp.pad(dt_bias.astype(F32), (0, pad_h))[None, :]
    alog = jnp.pad(a_log.astype(F32), (0, pad_h))[None, :]
    cwx, cwbc = conv_w[:, :SSD_D_INNER], conv_w[:, SSD_D_INNER:]
    cbx, cbbc = conv_b[None, :SSD_D_INNER], conv_b[None, SSD_D_INNER:]
    chunk = lambda b, c: b * nc + c
    full = lambda shape: pl.BlockSpec(shape, lambda b, c: (0, 0))
    return pl.pallas_call(
        _ssd_kernel,
        grid=(bsz, nc),
        in_specs=[pl.BlockSpec((L, SSD_D_INNER), lambda b, c: (chunk(b, c), 0)),
                  pl.BlockSpec((L, SSD_D_INNER), lambda b, c: (chunk(b, c), 1)),
                  pl.BlockSpec((L, SSD_BC_DIM), lambda b, c: (chunk(b, c), 4)),
                  pl.BlockSpec((L, LANES), lambda b, c: (chunk(b, c), 0)),
                  full((SSD_CONV_WIDTH, SSD_D_INNER)), full((SSD_CONV_WIDTH, SSD_BC_DIM)),
                  full((1, SSD_D_INNER)), full((1, SSD_BC_DIM)),
                  full((1, LANES)), full((1, LANES)),
                  full((1, SSD_D_INNER)), full((1, SSD_D_INNER)),
                  full((LANES, SSD_D_INNER))],
        out_specs=pl.BlockSpec((L, SSD_D_INNER), lambda b, c: (chunk(b, c), 0)),
        out_shape=jax.ShapeDtypeStruct((bsz * seq, SSD_D_INNER), F32),
        scratch_shapes=[pltpu.VMEM((L + 8, SSD_D_INNER), F32),
                        pltpu.VMEM((L + 8, SSD_BC_DIM), F32),
                        pltpu.VMEM((SSD_D_STATE, SSD_D_INNER), F32),
                        pltpu.VMEM((L, SSD_D_INNER), F32)],
        name="ssd_core",
        compiler_params=_params(("parallel", "arbitrary")),
    )(zxbc, zxbc, zxbc, dtp, cwx, cwbc, cbx, cbbc, dtb, alog, dexp, norm_g[None, :], expand)


def _moba_kernel(slopes_ref, qt_ref, k_ref, vt_ref, o_ref, kmean, sel_buf, logit_buf, max_buf, sum_buf):
    hp = pl.program_id(1)
    qi = pl.program_id(2)
    Q, KB, DH = MOBA_Q_BLOCK, MOBA_BLOCK, MOBA_HEAD_DIM
    n_blk = k_ref.shape[0] // KB
    own = (qi * Q) // KB
    q0 = qi * Q
    scale = DH ** -0.5

    @pl.when(qi == 0)
    def _():
        for n in range(n_blk):
            kmean[n:n + 1, :] = jnp.mean(k_ref[n * KB:(n + 1) * KB, :], axis=0, keepdims=True)

    qt = qt_ref[0]
    feat = lax.broadcasted_iota(jnp.int32, (2 * DH, Q), 0)
    q_bd = jnp.concatenate([jnp.where(feat < DH, qt, 0.0), jnp.where(feat >= DH, qt, 0.0)], axis=1)
    q_bd = q_bd.astype(BF16)
    blk = lax.broadcasted_iota(jnp.int32, (n_blk, 2 * Q), 0)
    gate = jnp.where(blk < own, _dot(kmean[...], q_bd), NEG_INF)
    rank = jnp.zeros((n_blk, 2 * Q), jnp.int32)
    for m in range(n_blk):
        gm = gate[m:m + 1, :]
        beats = (gm > gate) | ((gm == gate) & (blk > m))
        rank = rank + beats.astype(jnp.int32)
    sel_buf[...] = ((rank < MOBA_TOPK) & (blk < own)).astype(F32)
    max_buf[...] = jnp.full(max_buf.shape, NEG_INF, F32)
    sum_buf[...] = jnp.zeros(sum_buf.shape, F32)
    slope = jnp.concatenate([jnp.full((1, Q), slopes_ref[2 * hp], F32),
                             jnp.full((1, Q), slopes_ref[2 * hp + 1], F32)], axis=1)
    qpos = lax.broadcasted_iota(jnp.int32, (KB, 2 * Q), 1) & (Q - 1)
    qk = qpos - lax.broadcasted_iota(jnp.int32, (KB, 2 * Q), 0)

    def fold(x):
        return x.reshape(KB // 8, 8, 2 * Q)

    def pass1(n, carry):
        start = pl.multiple_of(n * KB, KB)
        dist = qk + (q0 - n * KB)
        keep = (sel_buf[pl.ds(n, 1), :] > 0.0) | ((dist >= 0) & (n == own))
        lg = _dot(k_ref[pl.ds(start, KB), :], q_bd) * scale - slope * dist.astype(F32)
        lg = jnp.where(keep, lg, NEG_INF)
        logit_buf[n] = lg
        max_buf[...] = jnp.maximum(max_buf[...], jnp.max(fold(lg), axis=0))
        return carry

    lax.fori_loop(0, own + 1, pass1, 0)
    col_max = jnp.max(max_buf[...], axis=0, keepdims=True)

    def pass2(n, accs):
        p = jnp.exp(logit_buf[n] - col_max)
        sum_buf[...] = sum_buf[...] + jnp.sum(fold(p), axis=0)
        vt = vt_ref[0, n]
        return tuple(accs[i] + _dot(vt[i * DH:(i + 1) * DH, :], p[:, i * Q:(i + 1) * Q]) for i in range(2))

    accs = lax.fori_loop(0, own + 1, pass2, (jnp.zeros((DH, Q), F32), jnp.zeros((DH, Q), F32)))
    denom = jnp.sum(sum_buf[...], axis=0, keepdims=True)
    for i in range(2):
        o_ref[0, i * DH:(i + 1) * DH, :] = accs[i] / denom[:, i * Q:(i + 1) * Q]


def moba_core(qkv, bsz, seq):
    Q, KB = MOBA_Q_BLOCK, MOBA_BLOCK
    nq, n_blk = seq // Q, seq // KB
    n_pairs = MOBA_N_HEADS // 2
    slopes = 2.0 ** (-8.0 * jnp.arange(1, MOBA_N_HEADS + 1, dtype=F32) / MOBA_N_HEADS)
    q_t = qkv[:, :D_MODEL].reshape(bsz, seq, D_MODEL).transpose(0, 2, 1)
    v_t = qkv[:, 2 * D_MODEL:].reshape(bsz, n_blk, KB, D_MODEL).transpose(0, 1, 3, 2)
    grid_spec = pltpu.PrefetchScalarGridSpec(
        num_scalar_prefetch=1,
        grid=(bsz, n_pairs, nq),
        in_specs=[pl.BlockSpec((1, LANES, Q), lambda b, h, i, s: (b, h, i)),
                  pl.BlockSpec((seq, LANES), lambda b, h, i, s: (b, n_pairs + h)),
                  pl.BlockSpec((1, n_blk, LANES, KB), lambda b, h, i, s: (b, 0, h, 0))],
        out_specs=pl.BlockSpec((1, LANES, Q), lambda b, h, i, s: (b, h, i)),
        scratch_shapes=[pltpu.VMEM((n_blk, LANES), F32),
                        pltpu.VMEM((n_blk, 2 * Q), F32),
                        pltpu.VMEM((n_blk, KB, 2 * Q), F32),
                        pltpu.VMEM((8, 2 * Q), F32), pltpu.VMEM((8, 2 * Q), F32)],
    )
    out_t = pl.pallas_call(
        _moba_kernel, grid_spec=grid_spec,
        out_shape=jax.ShapeDtypeStruct((bsz, D_MODEL, seq), F32),
        name="moba_core",
        compiler_params=_params(("parallel", "parallel", "arbitrary")),
    )(slopes, q_t, qkv, v_t)
    return out_t.transpose(0, 2, 1).reshape(bsz * seq, D_MODEL)


CONV_ROWS = 256


def _dwconv_kernel(h_ref, w_ref, b_ref, o_ref, pad):
    seq = h_ref.shape[1]
    pad[0:CONV_PAD_ROWS, :] = jnp.zeros((CONV_PAD_ROWS, pad.shape[1]), F32)
    pad[CONV_PAD_ROWS:CONV_PAD_ROWS + seq, :] = h_ref[0]
    first = CONV_PAD_ROWS - (CONV_KERNEL - 1)
    for c in range(seq // CONV_ROWS):
        r0 = c * CONV_ROWS
        acc = jnp.broadcast_to(b_ref[...], (CONV_ROWS, pad.shape[1]))
        for k in range(CONV_KERNEL):
            acc = acc + w_ref[k:k + 1, :] * pad[r0 + first + k:r0 + first + k + CONV_ROWS, :]
        o_ref[0, r0:r0 + CONV_ROWS, :] = acc


def dwconv(h, w, b, bsz, seq, tc=256):
    chans = h.shape[1]
    h3 = h.reshape(bsz, seq, chans)
    out = pl.pallas_call(
        _dwconv_kernel,
        grid=(bsz, chans // tc),
        in_specs=[pl.BlockSpec((1, seq, tc), lambda b, c: (b, 0, c)),
                  pl.BlockSpec((CONV_KERNEL, tc), lambda b, c: (0, c)),
                  pl.BlockSpec((1, tc), lambda b, c: (0, c))],
        out_specs=pl.BlockSpec((1, seq, tc), lambda b, c: (b, 0, c)),
        out_shape=jax.ShapeDtypeStruct((bsz, seq, chans), F32),
        scratch_shapes=[pltpu.VMEM((CONV_PAD_ROWS + seq, tc), F32)],
        name="dwconv",
        compiler_params=_params(("parallel", "parallel")),
    )(h3, w, b[None, :])
    return out.reshape(bsz * seq, chans)


PEER_CAND_PER_A = tuple(PEER_TOPK // (a + 1) for a in range(PEER_TOPK))
PEER_CAND_ROWS = 56


def _top16_rows(cur, dst_val, dst_idx, payload=None):
    rows = cur.shape[0]
    riota = lax.broadcasted_iota(jnp.int32, cur.shape, 0).astype(F32)
    for i in range(PEER_TOPK):
        m = jnp.max(cur, axis=0, keepdims=True)
        am = jnp.min(jnp.where(cur == m, riota, float(rows)), axis=0, keepdims=True)
        hit = riota == am
        dst_val[i:i + 1, :] = m
        if payload is None:
            dst_idx[i:i + 1, :] = am
        else:
            dst_idx[i:i + 1, :] = jnp.max(jnp.where(hit, payload, -1.0), axis=0, keepdims=True)
        cur = jnp.where(hit, NEG_INF, cur)


def _peer_route_kernel(x_ref, wq_ref, sk_ref, eidx_ref, gate_ref, v1, i1, v2, i2, cand, cidx, best, bidx):
    K = PEER_TOPK
    q = _dot(x_ref[...], wq_ref[...])
    cand[PEER_CAND_ROWS - 8:PEER_CAND_ROWS, :] = jnp.full((8, cand.shape[1]), NEG_INF, F32)
    cidx[PEER_CAND_ROWS - 8:PEER_CAND_ROWS, :] = jnp.zeros((8, cand.shape[1]), F32)
    for h in range(PEER_HEADS):
        for c, (vv, ii) in enumerate(((v1, i1), (v2, i2))):
            hc = 2 * h + c
            s_t = _dot_nt(sk_ref[hc], q[:, hc * PEER_HALF:(hc + 1) * PEER_HALF])
            _top16_rows(s_t, vv, ii)
        off = 0
        for a, n_b in enumerate(PEER_CAND_PER_A):
            cand[off:off + n_b, :] = v1[a:a + 1, :] + v2[0:n_b, :]
            cidx[off:off + n_b, :] = i1[a:a + 1, :] * float(PEER_N_KEYS) + i2[0:n_b, :]
            off += n_b
        _top16_rows(cand[...], best, bidx, payload=cidx[...])
        eidx_ref[h * K:(h + 1) * K, :] = bidx[...].astype(jnp.int32)
        b = best[...]
        e = jnp.exp(b - b[0:1, :])
        gate_ref[h * K:(h + 1) * K, :] = e / jnp.sum(e, axis=0, keepdims=True)


def peer_route(x, wq, sk, tm=256):
    t, d = x.shape
    tm = min(tm, t)
    K = PEER_TOPK
    out_spec = pl.BlockSpec((PEER_PICKS, tm), lambda i: (0, i))
    return pl.pallas_call(
        _peer_route_kernel,
        grid=(t // tm,),
        in_specs=[pl.BlockSpec((tm, d), lambda i: (i, 0)),
                  pl.BlockSpec(wq.shape, lambda i: (0, 0)),
                  pl.BlockSpec(sk.shape, lambda i: (0, 0, 0))],
        out_specs=[out_spec, out_spec],
        out_shape=[jax.ShapeDtypeStruct((PEER_PICKS, t), jnp.int32),
                   jax.ShapeDtypeStruct((PEER_PICKS, t), F32)],
        scratch_shapes=[pltpu.VMEM((K, tm), F32)] * 4
                       + [pltpu.VMEM((PEER_CAND_ROWS, tm), F32)] * 2
                       + [pltpu.VMEM((K, tm), F32)] * 2,
        name="peer_route",
        compiler_params=_params(("parallel",)),
    )(x, wq, sk)


PEER_STEP_TOKENS = 32
PEER_HALF_TOKENS = PEER_STEP_TOKENS // 2
CHUNKS = D_MODEL // LANES
WORD_ROWS = CHUNKS // 2
PEER_VMEM_LIMIT = 44 * 1024 * 1024


def _gelu_tanh(x):
    return 0.5 * x * (1.0 + jnp.tanh(math.sqrt(2.0 / math.pi) * (x + 0.044715 * (x * x * x))))


def pack_expert_table(w):
    n = w.shape[0]
    bits = lax.bitcast_convert_type(w.astype(jnp.bfloat16), jnp.uint16).astype(jnp.uint32)
    bits = bits.reshape(n, WORD_ROWS, 2, LANES)
    words = lax.bitcast_convert_type(bits[:, :, 0, :] | (bits[:, :, 1, :] << 16), jnp.int32)
    return words.reshape(n * WORD_ROWS, LANES)


def _load_table(tab_hbm, tab, sem):
    @pl.when(pl.program_id(0) == 0)
    def _():
        cp = pltpu.make_async_copy(tab_hbm, tab, sem)
        cp.start()
        cp.wait()


def _gather_rows(ids, t, tab, stage):
    for j in range(PEER_PICKS):
        stage[j * WORD_ROWS:(j + 1) * WORD_ROWS, :] = tab[pl.ds(pl.multiple_of(ids[t, j], WORD_ROWS), WORD_ROWS), :]


def _chunk_mask():
    sub = lax.broadcasted_iota(jnp.int32, (CHUNKS, PEER_PICKS * CHUNKS), 0)
    lane = lax.broadcasted_iota(jnp.int32, (CHUNKS, PEER_PICKS * CHUNKS), 1)
    return sub == (lane & (CHUNKS - 1))


def _for_each_token(idx_hbm, ibuf, isem, tab, stages, finish):
    i = pl.program_id(0)
    n = pl.num_programs(0)
    H = PEER_HALF_TOKENS

    def ids_copy(half, bank):
        return pltpu.make_async_copy(idx_hbm.at[pl.ds(half * H, H)], ibuf.at[bank], isem.at[bank])

    @pl.when(i == 0)
    def _():
        ids_copy(0, 0).start()

    ids_copy(2 * i, 0).wait()
    ids_copy(2 * i + 1, 1).start()
    _gather_rows(ibuf.at[0], 0, tab, stages[0])
    for t in range(PEER_STEP_TOKENS):
        nxt = min(t + 1, PEER_STEP_TOKENS - 1)
        if t == H - 1:
            ids_copy(2 * i + 1, 1).wait()

            @pl.when(i + 1 < n)
            def _():
                ids_copy(2 * i + 2, 0).start()
        _gather_rows(ibuf.at[nxt // H], nxt % H, tab, stages[(t + 1) % 2])
        finish(t, pltpu.bitcast(stages[t % 2][...], jnp.bfloat16))


def _peer_act_kernel(idx_hbm, x_ref, gate_ref, tab_hbm, w_ref, tab, stage0, stage1, acts, sem, ibuf, isem):
    _load_table(tab_hbm, tab, sem)
    mask = _chunk_mask()

    def finish(t, rows):
        r = _dot_nt(x_ref[t], rows)
        acts[t:t + 1, :] = jnp.sum(jnp.where(mask, r, 0.0), axis=0, keepdims=True)

    _for_each_token(idx_hbm, ibuf, isem, tab, (stage0, stage1), finish)
    r_id = lax.broadcasted_iota(jnp.int32, (LANES, LANES), 0) // CHUNKS
    c_id = lax.broadcasted_iota(jnp.int32, (LANES, LANES), 1) // CHUNKS
    group_ones = (r_id == c_id).astype(F32)
    for c in range(CHUNKS):
        sl = (slice(None), slice(c * LANES, (c + 1) * LANES))
        w_ref[sl] = gate_ref[sl] * _gelu_tanh(_dot_exact(acts[sl], group_ones))


def _peer_mix_kernel(idx_hbm, w_ref, x_ref, g_ref, b_ref, tab_hbm, o_ref, tab, stage0, stage1, sem, ibuf, isem):
    _load_table(tab_hbm, tab, sem)
    mask = _chunk_mask()

    def finish(t, rows):
        wrow = jnp.broadcast_to(w_ref[t:t + 1, :], mask.shape)
        o_ref[t] = DEEPNORM_ALPHA * x_ref[t] + _dot(jnp.where(mask, wrow, 0.0), rows)

    _for_each_token(idx_hbm, ibuf, isem, tab, (stage0, stage1), finish)
    y = o_ref[...]
    mu = jnp.mean(jnp.mean(y, axis=2, keepdims=True), axis=1, keepdims=True)
    yc = y - mu
    var = jnp.mean(jnp.mean(yc * yc, axis=2, keepdims=True), axis=1, keepdims=True)
    o_ref[...] = yc * lax.rsqrt(var + LN_EPS) * g_ref[...][None] + b_ref[...][None]


def peer_experts(x, eidx_t, gates_t, u_tab, v_tab, g, b):
    t, d = x.shape
    TB, P = PEER_STEP_TOKENS, PEER_PICKS
    n = t // TB
    x3 = x.reshape(t, CHUNKS, LANES)
    gate_rep = jnp.repeat(gates_t, CHUNKS, axis=1)
    x_spec = pl.BlockSpec((TB, CHUNKS, LANES), lambda i: (i, 0, 0))
    wide_spec = pl.BlockSpec((TB, P * CHUNKS), lambda i: (i, 0))
    chunk_row = pl.BlockSpec((CHUNKS, LANES), lambda i: (0, 0))
    hbm = pl.BlockSpec(memory_space=pl.ANY)
    scratch = [pltpu.VMEM(u_tab.shape, jnp.int32),
               pltpu.VMEM((P * WORD_ROWS, LANES), jnp.int32), pltpu.VMEM((P * WORD_ROWS, LANES), jnp.int32)]
    ids_scratch = [pltpu.SMEM((2, PEER_HALF_TOKENS, P), jnp.int32), pltpu.SemaphoreType.DMA((2,))]
    params = pltpu.CompilerParams(dimension_semantics=("arbitrary",), vmem_limit_bytes=PEER_VMEM_LIMIT)
    w = pl.pallas_call(
        _peer_act_kernel,
        grid=(n,),
        in_specs=[hbm, x_spec, wide_spec, hbm],
        out_specs=wide_spec,
        out_shape=jax.ShapeDtypeStruct((t, P * CHUNKS), F32),
        scratch_shapes=scratch + [pltpu.VMEM((TB, P * CHUNKS), F32), pltpu.SemaphoreType.DMA(())] + ids_scratch,
        name="peer_act",
        compiler_params=params,
    )(eidx_t, x3, gate_rep, u_tab)
    out = pl.pallas_call(
        _peer_mix_kernel,
        grid=(n,),
        in_specs=[hbm, wide_spec, x_spec, chunk_row, chunk_row, hbm],
        out_specs=x_spec,
        out_shape=jax.ShapeDtypeStruct((t, CHUNKS, LANES), F32),
        scratch_shapes=scratch + [pltpu.SemaphoreType.DMA(())] + ids_scratch,
        name="peer_mix",
        compiler_params=params,
    )(eidx_t, w, x3, g.reshape(CHUNKS, LANES), b.reshape(CHUNKS, LANES), v_tab)
    return out.reshape(t, d)


def _row(v):
    return v.astype(F32)[None, :]


def kernel(x, p, ssd_w_in, ssd_conv_w, ssd_conv_b, ssd_dt_bias, ssd_a_log, ssd_d, ssd_norm_g, ssd_w_out, moba_w_qkv, moba_w_out, conv_w_pw1, conv_b_pw1, conv_w_dw, conv_b_dw, conv_ln_g, conv_ln_b, conv_w_pw2, peer_w_q, peer_sub_keys, peer_u, peer_v, ln_mix_g, ln_mix_b, ln_ffn_g, ln_ffn_b, ple_w_gate, ple_w_proj):
    bsz, seq, d = x.shape
    t = bsz * seq
    x = x.reshape(t, d)
    zx_end = SSD_D_INNER + SSD_D_INNER + SSD_BC_DIM
    for i in range(DEPTH):
        kind, j = i % N_MIXERS, i // N_MIXERS
        g_mix, b_mix = _row(ln_mix_g[i]), _row(ln_mix_b[i])
        if kind == 0:
            w_in = ssd_w_in[j]
            w_dt = jnp.pad(w_in[:, zx_end:], ((0, 0), (0, LANES - SSD_N_HEADS)))
            zxbc = matmul(x, w_in[:, :zx_end].astype(BF16))
            dtp = matmul(x, w_dt.astype(BF16))
            y = ssd_core(zxbc, dtp, ssd_conv_w[j], ssd_conv_b[j], ssd_dt_bias[j], ssd_a_log[j], ssd_d[j],
                         ssd_norm_g[j], bsz, seq)
            x = matmul_ln(y, ssd_w_out[j].astype(BF16), x, g_mix, b_mix)
        elif kind == 1:
            qkv = matmul(x, moba_w_qkv[j].astype(BF16))
            att = moba_core(qkv, bsz, seq)
            x = matmul_ln(att, moba_w_out[j].astype(BF16), x, g_mix, b_mix)
        else:
            w1 = conv_w_pw1[j].astype(BF16)
            h = matmul_glu(x, w1[:, :D_MODEL], w1[:, D_MODEL:],
                           _row(conv_b_pw1[j][:D_MODEL]), _row(conv_b_pw1[j][D_MODEL:]))
            h = dwconv(h, conv_w_dw[j], conv_b_dw[j], bsz, seq)
            x = matmul_ln(h, conv_w_pw2[j].astype(BF16), x, g_mix, b_mix,
                          pre=(_row(conv_ln_g[j]), _row(conv_ln_b[j])))
        sk = peer_sub_keys[i].reshape(PEER_HEADS * 2, PEER_N_KEYS, PEER_HALF).astype(BF16)
        eidx, gates = peer_route(x, peer_w_q[i].astype(BF16), sk)
        x = peer_experts(x, eidx.T * WORD_ROWS, gates.T, pack_expert_table(peer_u[i]), pack_expert_table(peer_v[i]),
                         ln_ffn_g[i].astype(F32), ln_ffn_b[i].astype(F32))
        x = ple_add(x, p[i].reshape(t, -1), ple_w_gate[i].astype(BF16), ple_w_proj[i].astype(BF16))
    return x.reshape(bsz, seq, d)
```

```python
import functools
import math

import jax
import jax.numpy as jnp
from jax import lax
from jax.experimental import pallas as pl
from jax.experimental.pallas import tpu as pltpu

F32 = jnp.float32
BF16 = jnp.bfloat16

D_MODEL = 1024
DEPTH = 4
N_MIXERS = 3

SSD_D_INNER = 2048
SSD_HEADDIM = 64
SSD_N_HEADS = 32
SSD_N_GROUPS = 4
SSD_HEADS_PER_GROUP = 8
SSD_D_STATE = 128
SSD_CONV_WIDTH = 4
SSD_CHUNK = 128
SSD_GROUP_LANES = SSD_D_INNER // SSD_N_GROUPS
SSD_BC_DIM = 2 * SSD_N_GROUPS * SSD_D_STATE

MOBA_HEAD_DIM = 64
MOBA_N_HEADS = 16
MOBA_BLOCK = 256
MOBA_TOPK = 3
MOBA_Q_BLOCK = 128

CONV_KERNEL = 31
CONV_PAD_ROWS = 32

PEER_N_KEYS = 128
PEER_HEADS = 8
PEER_TOPK = 16
PEER_QUERY_DIM = 256
PEER_HALF = 128
PEER_PICKS = PEER_HEADS * PEER_TOPK

LN_EPS = 1e-5
DEEPNORM_ALPHA = (2 * DEPTH) ** 0.25

LANES = 128
VMEM_LIMIT = 48 * 1024 * 1024

NEG_INF = float("-inf")


def _params(sem):
    return pltpu.CompilerParams(dimension_semantics=sem, vmem_limit_bytes=VMEM_LIMIT)


def _layer_norm(x, g, b):
    mu = jnp.mean(x, axis=-1, keepdims=True)
    xc = x - mu
    var = jnp.mean(xc * xc, axis=-1, keepdims=True)
    return xc * lax.rsqrt(var + LN_EPS) * g + b


def _sigmoid(x):
    return 1.0 / (1.0 + jnp.exp(-x))


def _silu(x):
    return x * _sigmoid(x)


def _dot(a, b):
    return jnp.dot(a.astype(BF16), b.astype(BF16), preferred_element_type=F32)


def _dot_nt(a, b):
    return lax.dot_general(a.astype(BF16), b.astype(BF16), (((1,), (1,)), ((), ())),
                           preferred_element_type=F32)


def _dot_exact(a, b):
    return jnp.dot(a, b, preferred_element_type=F32, precision=lax.Precision.HIGHEST)


def _mm_kernel(a_ref, w_ref, o_ref):
    o_ref[...] = _dot(a_ref[...], w_ref[...])


def matmul(a, w, tm=1024, tn=512):
    m, k = a.shape
    n = w.shape[1]
    tm, tn = min(tm, m), min(tn, n)
    return pl.pallas_call(
        _mm_kernel,
        grid=(m // tm, n // tn),
        in_specs=[pl.BlockSpec((tm, k), lambda i, j: (i, 0)),
                  pl.BlockSpec((k, tn), lambda i, j: (0, j))],
        out_specs=pl.BlockSpec((tm, tn), lambda i, j: (i, j)),
        out_shape=jax.ShapeDtypeStruct((m, n), F32),
        name="mm",
        compiler_params=_params(("parallel", "parallel")),
    )(a, w)


def _mm_glu_kernel(a_ref, w1_ref, w2_ref, b1_ref, b2_ref, o_ref):
    a = a_ref[...].astype(BF16)
    h1 = jnp.dot(a, w1_ref[...], preferred_element_type=F32) + b1_ref[...]
    h2 = jnp.dot(a, w2_ref[...], preferred_element_type=F32) + b2_ref[...]
    o_ref[...] = h1 * _sigmoid(h2)


def matmul_glu(a, w1, w2, b1, b2, tm=512, tn=512):
    m, k = a.shape
    n = w1.shape[1]
    tm, tn = min(tm, m), min(tn, n)
    wspec = pl.BlockSpec((k, tn), lambda j, i: (0, j))
    bspec = pl.BlockSpec((1, tn), lambda j, i: (0, j))
    return pl.pallas_call(
        _mm_glu_kernel,
        grid=(n // tn, m // tm),
        in_specs=[pl.BlockSpec((tm, k), lambda j, i: (i, 0)), wspec, wspec, bspec, bspec],
        out_specs=pl.BlockSpec((tm, tn), lambda j, i: (i, j)),
        out_shape=jax.ShapeDtypeStruct((m, n), F32),
        name="mm_glu",
        compiler_params=_params(("parallel", "parallel")),
    )(a, w1, w2, b1, b2)


def _mm_ln_kernel(a_ref, w_ref, res_ref, g_ref, b_ref, o_ref):
    mix = _dot(a_ref[...], w_ref[...])
    o_ref[...] = _layer_norm(DEEPNORM_ALPHA * res_ref[...] + mix, g_ref[...], b_ref[...])


def _mm_lnsilu_ln_kernel(a_ref, pg_ref, pb_ref, w_ref, res_ref, g_ref, b_ref, o_ref):
    h = _silu(_layer_norm(a_ref[...], pg_ref[...], pb_ref[...]))
    mix = _dot(h, w_ref[...])
    o_ref[...] = _layer_norm(DEEPNORM_ALPHA * res_ref[...] + mix, g_ref[...], b_ref[...])


def matmul_ln(a, w, res, g, b, pre=None, tm=256):
    m, k = a.shape
    n = w.shape[1]
    tm = min(tm, m)
    row = lambda width: pl.BlockSpec((1, width), lambda i: (0, 0))
    a_spec = pl.BlockSpec((tm, k), lambda i: (i, 0))
    w_spec = pl.BlockSpec((k, n), lambda i: (0, 0))
    io_spec = pl.BlockSpec((tm, n), lambda i: (i, 0))
    if pre is None:
        body, ins, args = _mm_ln_kernel, [a_spec, w_spec, io_spec, row(n), row(n)], (a, w, res, g, b)
    else:
        body = _mm_lnsilu_ln_kernel
        ins = [a_spec, row(k), row(k), w_spec, io_spec, row(n), row(n)]
        args = (a, pre[0], pre[1], w, res, g, b)
    return pl.pallas_call(
        body, grid=(m // tm,), in_specs=ins, out_specs=io_spec,
        out_shape=jax.ShapeDtypeStruct((m, n), F32),
        name="mm_ln",
        compiler_params=_params(("parallel",)),
    )(*args)


def _ple_kernel(x_ref, p_ref, wg_ref, wp_ref, xres_ref, o_ref):
    gate = _sigmoid(_dot(x_ref[...], wg_ref[...]))
    o_ref[...] = xres_ref[...] + gate * _dot(p_ref[...], wp_ref[...])


def ple_add(x, p, wg, wp, tm=512, tn=512):
    m, k = x.shape
    kp = p.shape[1]
    n = wg.shape[1]
    tm, tn = min(tm, m), min(tn, n)
    return pl.pallas_call(
        _ple_kernel,
        grid=(n // tn, m // tm),
        in_specs=[pl.BlockSpec((tm, k), lambda j, i: (i, 0)),
                  pl.BlockSpec((tm, kp), lambda j, i: (i, 0)),
                  pl.BlockSpec((k, tn), lambda j, i: (0, j)),
                  pl.BlockSpec((kp, tn), lambda j, i: (0, j)),
                  pl.BlockSpec((tm, tn), lambda j, i: (i, j))],
        out_specs=pl.BlockSpec((tm, tn), lambda j, i: (i, j)),
        out_shape=jax.ShapeDtypeStruct((m, n), F32),
        name="ple_add",
        compiler_params=_params(("parallel", "parallel")),
    )(x, p, wg, wp, x)


def _ssd_kernel(z_ref, x_ref, bc_ref, dt_ref, cwx_ref, cwbc_ref, cbx_ref, cbbc_ref, dtb_ref, alog_ref,
                dexp_ref, ng_ref, expand_ref, o_ref, xpad, bcpad, state, yacc):
    L = SSD_CHUNK
    first = pl.program_id(1) == 0

    @pl.when(first)
    def _():
        xpad[0:8, :] = jnp.zeros((8, SSD_D_INNER), F32)
        bcpad[0:8, :] = jnp.zeros((8, SSD_BC_DIM), F32)
        state[...] = jnp.zeros_like(state)

    xpad[8:8 + L, :] = x_ref[...]
    bcpad[8:8 + L, :] = bc_ref[...]
    xs = cbx_ref[...]
    bc = cbbc_ref[...]
    for k in range(SSD_CONV_WIDTH):
        lo = 8 - (SSD_CONV_WIDTH - 1) + k
        xs = xs + cwx_ref[k:k + 1, :] * xpad[lo:lo + L, :]
        bc = bc + cwbc_ref[k:k + 1, :] * bcpad[lo:lo + L, :]
    xs = _silu(xs)
    bc = _silu(bc)
    xpad[0:8, :] = x_ref[L - 8:L, :]
    bcpad[0:8, :] = bc_ref[L - 8:L, :]

    lane = lax.broadcasted_iota(jnp.int32, (1, LANES), 1)
    dt_in = dt_ref[...] + dtb_ref[...]
    dt = jnp.maximum(dt_in, 0.0) + jnp.log1p(jnp.exp(-jnp.abs(dt_in)))
    dt = jnp.where(lane < SSD_N_HEADS, dt, 0.0)
    a = -jnp.exp(alog_ref[...])
    row = lax.broadcasted_iota(jnp.int32, (L, L), 0)
    col = lax.broadcasted_iota(jnp.int32, (L, L), 1)
    causal = col <= row
    a_cum = _dot_exact(causal.astype(F32), dt * a)
    a_cum_t = a_cum.T
    expand = expand_ref[...]
    dt_e = _dot_exact(dt, expand)
    acum_e = _dot_exact(a_cum, expand)
    alast_e = acum_e[L - 1:L, :]
    x_dt = xs * dt_e
    x_end = jnp.exp(alast_e - acum_e) * x_dt
    decay_in = jnp.exp(acum_e)

    lane_l = lax.broadcasted_iota(jnp.int32, (L, LANES), 1)
    for g in range(SSD_N_GROUPS):
        gl = slice(g * SSD_GROUP_LANES, (g + 1) * SSD_GROUP_LANES)
        b_g = bc[:, g * SSD_D_STATE:(g + 1) * SSD_D_STATE]
        c_g = bc[:, (SSD_N_GROUPS + g) * SSD_D_STATE:(SSD_N_GROUPS + g + 1) * SSD_D_STATE]
        cb = _dot_nt(c_g, b_g)
        prev = state[:, gl]
        yacc[:, gl] = _dot(c_g, prev) * decay_in[:, gl]
        new_states = _dot(b_g.T, x_end[:, gl])
        state[:, gl] = jnp.exp(alast_e[:, gl]) * prev + new_states
        for pr in range(SSD_HEADS_PER_GROUP // 2):
            ms = []
            for sub in range(2):
                h = g * SSD_HEADS_PER_GROUP + 2 * pr + sub
                seg = a_cum[:, h:h + 1] - a_cum_t[h:h + 1, :]
                ms.append(cb * jnp.exp(jnp.where(causal, seg, NEG_INF)))
            m_cat = jnp.concatenate(ms, axis=1)
            pl_ = slice(g * SSD_GROUP_LANES + pr * LANES, g * SSD_GROUP_LANES + (pr + 1) * LANES)
            xp = x_dt[:, pl_]
            x_bd = jnp.concatenate([jnp.where(lane_l < SSD_HEADDIM, xp, 0.0),
                                    jnp.where(lane_l >= SSD_HEADDIM, xp, 0.0)], axis=0)
            yacc[:, pl_] = yacc[:, pl_] + _dot(m_cat, x_bd)

    y = yacc[...] + dexp_ref[...] * xs
    gated = y * _silu(z_ref[...])
    for g in range(SSD_N_GROUPS):
        gl = slice(g * SSD_GROUP_LANES, (g + 1) * SSD_GROUP_LANES)
        gg = gated[:, gl]
        ms = jnp.mean(gg * gg, axis=-1, keepdims=True)
        o_ref[:, gl] = gg * lax.rsqrt(ms + LN_EPS) * ng_ref[:, gl]


def ssd_core(zxbc, dtp, conv_w, conv_b, dt_bias, a_log, d_skip, norm_g, bsz, seq):
    L = SSD_CHUNK
    nc = seq // L
    pad_h = LANES - SSD_N_HEADS
    head_of_lane = jnp.arange(SSD_D_INNER, dtype=jnp.int32) // SSD_HEADDIM
    expand = (jnp.arange(LANES, dtype=jnp.int32)[:, None] == head_of_lane[None, :]).astype(F32)
    dexp = jnp.repeat(d_skip.astype(F32), SSD_HEADDIM)[None, :]
    dtb = jnp.pad(dt_bias.astype(F32), (0, pad_h))[None, :]
    alog = jnp.pad(a_log.astype(F32), (0, pad_h))[None, :]
    cwx, cwbc = conv_w[:, :SSD_D_INNER], conv_w[:, SSD_D_INNER:]
    cbx, cbbc = conv_b[None, :SSD_D_INNER], conv_b[None, SSD_D_INNER:]
    chunk = lambda b, c: b * nc + c
    full = lambda shape: pl.BlockSpec(shape, lambda b, c: (0, 0))
    return pl.pallas_call(
        _ssd_kernel,
        grid=(bsz, nc),
        in_specs=[pl.BlockSpec((L, SSD_D_INNER), lambda b, c: (chunk(b, c), 0)),
                  pl.BlockSpec((L, SSD_D_INNER), lambda b, c: (chunk(b, c), 1)),
                  pl.BlockSpec((L, SSD_BC_DIM), lambda b, c: (chunk(b, c), 4)),
                  pl.BlockSpec((L, LANES), lambda b, c: (chunk(b, c), 0)),
                  full((SSD_CONV_WIDTH, SSD_D_INNER)), full((SSD_CONV_WIDTH, SSD_BC_DIM)),
                  full((1, SSD_D_INNER)), full((1, SSD_BC_DIM)),
                  full((1, LANES)), full((1, LANES)),
                  full((1, SSD_D_INNER)), full((1, SSD_D_INNER)),
                  full((LANES, SSD_D_INNER))],
        out_specs=pl.BlockSpec((L, SSD_D_INNER), lambda b, c: (chunk(b, c), 0)),
        out_shape=jax.ShapeDtypeStruct((bsz * seq, SSD_D_INNER), F32),
        scratch_shapes=[pltpu.VMEM((L + 8, SSD_D_INNER), F32),
                        pltpu.VMEM((L + 8, SSD_BC_DIM), F32),
                        pltpu.VMEM((SSD_D_STATE, SSD_D_INNER), F32),
                        pltpu.VMEM((L, SSD_D_INNER), F32)],
        name="ssd_core",
        compiler_params=_params(("parallel", "arbitrary")),
    )(zxbc, zxbc, zxbc, dtp, cwx, cwbc, cbx, cbbc, dtb, alog, dexp, norm_g[None, :], expand)


def _moba_kernel(slopes_ref, qt_ref, k_ref, vt_ref, o_ref, kmean, sel_buf, logit_buf, max_buf, sum_buf):
    hp = pl.program_id(1)
    qi = pl.program_id(2)
    Q, KB, DH = MOBA_Q_BLOCK, MOBA_BLOCK, MOBA_HEAD_DIM
    n_blk = k_ref.shape[0] // KB
    own = (qi * Q) // KB
    q0 = qi * Q
    scale = DH ** -0.5

    @pl.when(qi == 0)
    def _():
        for n in range(n_blk):
            kmean[n:n + 1, :] = jnp.mean(k_ref[n * KB:(n + 1) * KB, :], axis=0, keepdims=True)

    qt = qt_ref[0]
    feat = lax.broadcasted_iota(jnp.int32, (2 * DH, Q), 0)
    q_bd = jnp.concatenate([jnp.where(feat < DH, qt, 0.0), jnp.where(feat >= DH, qt, 0.0)], axis=1)
    q_bd = q_bd.astype(BF16)
    blk = lax.broadcasted_iota(jnp.int32, (n_blk, 2 * Q), 0)
    gate = jnp.where(blk < own, _dot(kmean[...], q_bd), NEG_INF)
    rank = jnp.zeros((n_blk, 2 * Q), jnp.int32)
    for m in range(n_blk):
        gm = gate[m:m + 1, :]
        beats = (gm > gate) | ((gm == gate) & (blk > m))
        rank = rank + beats.astype(jnp.int32)
    sel_buf[...] = ((rank < MOBA_TOPK) & (blk < own)).astype(F32)
    max_buf[...] = jnp.full(max_buf.shape, NEG_INF, F32)
    sum_buf[...] = jnp.zeros(sum_buf.shape, F32)
    slope = jnp.concatenate([jnp.full((1, Q), slopes_ref[2 * hp], F32),
                             jnp.full((1, Q), slopes_ref[2 * hp + 1], F32)], axis=1)
    qpos = lax.broadcasted_iota(jnp.int32, (KB, 2 * Q), 1) & (Q - 1)
    qk = qpos - lax.broadcasted_iota(jnp.int32, (KB, 2 * Q), 0)

    def fold(x):
        return x.reshape(KB // 8, 8, 2 * Q)

    def pass1(n, carry):
        start = pl.multiple_of(n * KB, KB)
        dist = qk + (q0 - n * KB)
        keep = (sel_buf[pl.ds(n, 1), :] > 0.0) | ((dist >= 0) & (n == own))
        lg = _dot(k_ref[pl.ds(start, KB), :], q_bd) * scale - slope * dist.astype(F32)
        lg = jnp.where(keep, lg, NEG_INF)
        logit_buf[n] = lg
        max_buf[...] = jnp.maximum(max_buf[...], jnp.max(fold(lg), axis=0))
        return carry

    lax.fori_loop(0, own + 1, pass1, 0)
    col_max = jnp.max(max_buf[...], axis=0, keepdims=True)

    def pass2(n, accs):
        p = jnp.exp(logit_buf[n] - col_max)
        sum_buf[...] = sum_buf[...] + jnp.sum(fold(p), axis=0)
        vt = vt_ref[0, n]
        return tuple(accs[i] + _dot(vt[i * DH:(i + 1) * DH, :], p[:, i * Q:(i + 1) * Q]) for i in range(2))

    accs = lax.fori_loop(0, own + 1, pass2, (jnp.zeros((DH, Q), F32), jnp.zeros((DH, Q), F32)))
    denom = jnp.sum(sum_buf[...], axis=0, keepdims=True)
    for i in range(2):
        o_ref[0, i * DH:(i + 1) * DH, :] = accs[i] / denom[:, i * Q:(i + 1) * Q]


def moba_core(qkv, bsz, seq):
    Q, KB = MOBA_Q_BLOCK, MOBA_BLOCK
    nq, n_blk = seq // Q, seq // KB
    n_pairs = MOBA_N_HEADS // 2
    slopes = 2.0 ** (-8.0 * jnp.arange(1, MOBA_N_HEADS + 1, dtype=F32) / MOBA_N_HEADS)
    q_t = qkv[:, :D_MODEL].reshape(bsz, seq, D_MODEL).transpose(0, 2, 1)
    v_t = qkv[:, 2 * D_MODEL:].reshape(bsz, n_blk, KB, D_MODEL).transpose(0, 1, 3, 2)
    grid_spec = pltpu.PrefetchScalarGridSpec(
        num_scalar_prefetch=1,
        grid=(bsz, n_pairs, nq),
        in_specs=[pl.BlockSpec((1, LANES, Q), lambda b, h, i, s: (b, h, i)),
                  pl.BlockSpec((seq, LANES), lambda b, h, i, s: (b, n_pairs + h)),
                  pl.BlockSpec((1, n_blk, LANES, KB), lambda b, h, i, s: (b, 0, h, 0))],
        out_specs=pl.BlockSpec((1, LANES, Q), lambda b, h, i, s: (b, h, i)),
        scratch_shapes=[pltpu.VMEM((n_blk, LANES), F32),
                        pltpu.VMEM((n_blk, 2 * Q), F32),
                        pltpu.VMEM((n_blk, KB, 2 * Q), F32),
                        pltpu.VMEM((8, 2 * Q), F32), pltpu.VMEM((8, 2 * Q), F32)],
    )
    out_t = pl.pallas_call(
        _moba_kernel, grid_spec=grid_spec,
        out_shape=jax.ShapeDtypeStruct((bsz, D_MODEL, seq), F32),
        name="moba_core",
        compiler_params=_params(("parallel", "parallel", "arbitrary")),
    )(slopes, q_t, qkv, v_t)
    return out_t.transpose(0, 2, 1).reshape(bsz * seq, D_MODEL)


CONV_ROWS = 256


def _dwconv_kernel(h_ref, w_ref, b_ref, o_ref, pad):
    seq = h_ref.shape[1]
    pad[0:CONV_PAD_ROWS, :] = jnp.zeros((CONV_PAD_ROWS, pad.shape[1]), F32)
    pad[CONV_PAD_ROWS:CONV_PAD_ROWS + seq, :] = h_ref[0]
    first = CONV_PAD_ROWS - (CONV_KERNEL - 1)
    for c in range(seq // CONV_ROWS):
        r0 = c * CONV_ROWS
        acc = jnp.broadcast_to(b_ref[...], (CONV_ROWS, pad.shape[1]))
        for k in range(CONV_KERNEL):
            acc = acc + w_ref[k:k + 1, :] * pad[r0 + first + k:r0 + first + k + CONV_ROWS, :]
        o_ref[0, r0:r0 + CONV_ROWS, :] = acc


def dwconv(h, w, b, bsz, seq, tc=256):
    chans = h.shape[1]
    h3 = h.reshape(bsz, seq, chans)
    out = pl.pallas_call(
        _dwconv_kernel,
        grid=(bsz, chans // tc),
        in_specs=[pl.BlockSpec((1, seq, tc), lambda b, c: (b, 0, c)),
                  pl.BlockSpec((CONV_KERNEL, tc), lambda b, c: (0, c)),
                  pl.BlockSpec((1, tc), lambda b, c: (0, c))],
        out_specs=pl.BlockSpec((1, seq, tc), lambda b, c: (b, 0, c)),
        out_shape=jax.ShapeDtypeStruct((bsz, seq, chans), F32),
        scratch_shapes=[pltpu.VMEM((CONV_PAD_ROWS + seq, tc), F32)],
        name="dwconv",
        compiler_params=_params(("parallel", "parallel")),
    )(h3, w, b[None, :])
    return out.reshape(bsz * seq, chans)


PEER_CAND_PER_A = tuple(PEER_TOPK // (a + 1) for a in range(PEER_TOPK))
PEER_CAND_ROWS = 56


def _top16_rows(cur, dst_val, dst_idx, payload=None):
    rows = cur.shape[0]
    riota = lax.broadcasted_iota(jnp.int32, cur.shape, 0).astype(F32)
    for i in range(PEER_TOPK):
        m = jnp.max(cur, axis=0, keepdims=True)
        am = jnp.min(jnp.where(cur == m, riota, float(rows)), axis=0, keepdims=True)
        hit = riota == am
        dst_val[i:i + 1, :] = m
        if payload is None:
            dst_idx[i:i + 1, :] = am
        else:
            dst_idx[i:i + 1, :] = jnp.max(jnp.where(hit, payload, -1.0), axis=0, keepdims=True)
        cur = jnp.where(hit, NEG_INF, cur)


def _peer_route_kernel(x_ref, wq_ref, sk_ref, eidx_ref, gate_ref, v1, i1, v2, i2, cand, cidx, best, bidx):
    K = PEER_TOPK
    q = _dot(x_ref[...], wq_ref[...])
    cand[PEER_CAND_ROWS - 8:PEER_CAND_ROWS, :] = jnp.full((8, cand.shape[1]), NEG_INF, F32)
    cidx[PEER_CAND_ROWS - 8:PEER_CAND_ROWS, :] = jnp.zeros((8, cand.shape[1]), F32)
    for h in range(PEER_HEADS):
        for c, (vv, ii) in enumerate(((v1, i1), (v2, i2))):
            hc = 2 * h + c
            s_t = _dot_nt(sk_ref[hc], q[:, hc * PEER_HALF:(hc + 1) * PEER_HALF])
            _top16_rows(s_t, vv, ii)
        off = 0
        for a, n_b in enumerate(PEER_CAND_PER_A):
            cand[off:off + n_b, :] = v1[a:a + 1, :] + v2[0:n_b, :]
            cidx[off:off + n_b, :] = i1[a:a + 1, :] * float(PEER_N_KEYS) + i2[0:n_b, :]
            off += n_b
        _top16_rows(cand[...], best, bidx, payload=cidx[...])
        eidx_ref[h * K:(h + 1) * K, :] = bidx[...].astype(jnp.int32)
        b = best[...]
        e = jnp.exp(b - b[0:1, :])
        gate_ref[h * K:(h + 1) * K, :] = e / jnp.sum(e, axis=0, keepdims=True)


def peer_route(x, wq, sk, tm=256):
    t, d = x.shape
    tm = min(tm, t)
    K = PEER_TOPK
    out_spec = pl.BlockSpec((PEER_PICKS, tm), lambda i: (0, i))
    return pl.pallas_call(
        _peer_route_kernel,
        grid=(t // tm,),
        in_specs=[pl.BlockSpec((tm, d), lambda i: (i, 0)),
                  pl.BlockSpec(wq.shape, lambda i: (0, 0)),
                  pl.BlockSpec(sk.shape, lambda i: (0, 0, 0))],
        out_specs=[out_spec, out_spec],
        out_shape=[jax.ShapeDtypeStruct((PEER_PICKS, t), jnp.int32),
                   jax.ShapeDtypeStruct((PEER_PICKS, t), F32)],
        scratch_shapes=[pltpu.VMEM((K, tm), F32)] * 4
                       + [pltpu.VMEM((PEER_CAND_ROWS, tm), F32)] * 2
                       + [pltpu.VMEM((K, tm), F32)] * 2,
        name="peer_route",
        compiler_params=_params(("parallel",)),
    )(x, wq, sk)


PEER_STEP_TOKENS = 64
PEER_HALF_TOKENS = PEER_STEP_TOKENS // 2
CHUNKS = D_MODEL // LANES
WORD_ROWS = CHUNKS // 2
PEER_VMEM_LIMIT = 44 * 1024 * 1024


def _gelu_tanh(x):
    return 0.5 * x * (1.0 + jnp.tanh(math.sqrt(2.0 / math.pi) * (x + 0.044715 * (x * x * x))))


def pack_expert_table(w):
    n = w.shape[0]
    bits = lax.bitcast_convert_type(w.astype(jnp.bfloat16), jnp.uint16).astype(jnp.uint32)
    bits = bits.reshape(n, WORD_ROWS, 2, LANES)
    words = lax.bitcast_convert_type(bits[:, :, 0, :] | (bits[:, :, 1, :] << 16), jnp.int32)
    return words.reshape(n * WORD_ROWS, LANES)


def _load_table(tab_hbm, tab, sem):
    @pl.when(pl.program_id(0) == 0)
    def _():
        cp = pltpu.make_async_copy(tab_hbm, tab, sem)
        cp.start()
        cp.wait()


def _gather_rows(ids, t, tab, stage):
    for j in range(PEER_PICKS):
        stage[j * WORD_ROWS:(j + 1) * WORD_ROWS, :] = tab[pl.ds(pl.multiple_of(ids[t, j], WORD_ROWS), WORD_ROWS), :]


def _chunk_mask():
    sub = lax.broadcasted_iota(jnp.int32, (CHUNKS, PEER_PICKS * CHUNKS), 0)
    lane = lax.broadcasted_iota(jnp.int32, (CHUNKS, PEER_PICKS * CHUNKS), 1)
    return sub == (lane & (CHUNKS - 1))


def _for_each_token(idx_hbm, ibuf, isem, tab, stages, finish):
    i = pl.program_id(0)
    n = pl.num_programs(0)
    H = PEER_HALF_TOKENS

    def ids_copy(half, bank):
        return pltpu.make_async_copy(idx_hbm.at[pl.ds(half * H, H)], ibuf.at[bank], isem.at[bank])

    @pl.when(i == 0)
    def _():
        ids_copy(0, 0).start()

    ids_copy(2 * i, 0).wait()
    ids_copy(2 * i + 1, 1).start()
    _gather_rows(ibuf.at[0], 0, tab, stages[0])
    for t in range(PEER_STEP_TOKENS):
        nxt = min(t + 1, PEER_STEP_TOKENS - 1)
        if t == H - 1:
            ids_copy(2 * i + 1, 1).wait()

            @pl.when(i + 1 < n)
            def _():
                ids_copy(2 * i + 2, 0).start()
        _gather_rows(ibuf.at[nxt // H], nxt % H, tab, stages[(t + 1) % 2])
        finish(t, pltpu.bitcast(stages[t % 2][...], jnp.bfloat16))


def _peer_act_kernel(idx_hbm, x_ref, gate_ref, tab_hbm, w_ref, tab, stage0, stage1, acts, sem, ibuf, isem):
    _load_table(tab_hbm, tab, sem)
    mask = _chunk_mask()

    def finish(t, rows):
        r = _dot_nt(x_ref[t], rows)
        acts[t:t + 1, :] = jnp.sum(jnp.where(mask, r, 0.0), axis=0, keepdims=True)

    _for_each_token(idx_hbm, ibuf, isem, tab, (stage0, stage1), finish)
    r_id = lax.broadcasted_iota(jnp.int32, (LANES, LANES), 0) // CHUNKS
    c_id = lax.broadcasted_iota(jnp.int32, (LANES, LANES), 1) // CHUNKS
    group_ones = (r_id == c_id).astype(F32)
    for c in range(CHUNKS):
        sl = (slice(None), slice(c * LANES, (c + 1) * LANES))
        w_ref[sl] = gate_ref[sl] * _gelu_tanh(_dot_exact(acts[sl], group_ones))


def _peer_mix_kernel(idx_hbm, w_ref, x_ref, g_ref, b_ref, tab_hbm, o_ref, tab, stage0, stage1, sem, ibuf, isem):
    _load_table(tab_hbm, tab, sem)
    mask = _chunk_mask()

    def finish(t, rows):
        wrow = jnp.broadcast_to(w_ref[t:t + 1, :], mask.shape)
        o_ref[t] = DEEPNORM_ALPHA * x_ref[t] + _dot(jnp.where(mask, wrow, 0.0), rows)

    _for_each_token(idx_hbm, ibuf, isem, tab, (stage0, stage1), finish)
    y = o_ref[...]
    mu = jnp.mean(jnp.mean(y, axis=2, keepdims=True), axis=1, keepdims=True)
    yc = y - mu
    var = jnp.mean(jnp.mean(yc * yc, axis=2, keepdims=True), axis=1, keepdims=True)
    o_ref[...] = yc * lax.rsqrt(var + LN_EPS) * g_ref[...][None] + b_ref[...][None]


def peer_experts(x, eidx_t, gates_t, u_tab, v_tab, g, b):
    t, d = x.shape
    TB, P = PEER_STEP_TOKENS, PEER_PICKS
    n = t // TB
    x3 = x.reshape(t, CHUNKS, LANES)
    gate_rep = jnp.repeat(gates_t, CHUNKS, axis=1)
    x_spec = pl.BlockSpec((TB, CHUNKS, LANES), lambda i: (i, 0, 0))
    wide_spec = pl.BlockSpec((TB, P * CHUNKS), lambda i: (i, 0))
    chunk_row = pl.BlockSpec((CHUNKS, LANES), lambda i: (0, 0))
    hbm = pl.BlockSpec(memory_space=pl.ANY)
    scratch = [pltpu.VMEM(u_tab.shape, jnp.int32),
               pltpu.VMEM((P * WORD_ROWS, LANES), jnp.int32), pltpu.VMEM((P * WORD_ROWS, LANES), jnp.int32)]
    ids_scratch = [pltpu.SMEM((2, PEER_HALF_TOKENS, P), jnp.int32), pltpu.SemaphoreType.DMA((2,))]
    params = pltpu.CompilerParams(dimension_semantics=("arbitrary",), vmem_limit_bytes=PEER_VMEM_LIMIT)
    w = pl.pallas_call(
        _peer_act_kernel,
        grid=(n,),
        in_specs=[hbm, x_spec, wide_spec, hbm],
        out_specs=wide_spec,
        out_shape=jax.ShapeDtypeStruct((t, P * CHUNKS), F32),
        scratch_shapes=scratch + [pltpu.VMEM((TB, P * CHUNKS), F32), pltpu.SemaphoreType.DMA(())] + ids_scratch,
        name="peer_act",
        compiler_params=params,
    )(eidx_t, x3, gate_rep, u_tab)
    out = pl.pallas_call(
        _peer_mix_kernel,
        grid=(n,),
        in_specs=[hbm, wide_spec, x_spec, chunk_row, chunk_row, hbm],
        out_specs=x_spec,
        out_shape=jax.ShapeDtypeStruct((t, CHUNKS, LANES), F32),
        scratch_shapes=scratch + [pltpu.SemaphoreType.DMA(())] + ids_scratch,
        name="peer_mix",
        compiler_params=params,
    )(eidx_t, w, x3, g.reshape(CHUNKS, LANES), b.reshape(CHUNKS, LANES), v_tab)
    return out.reshape(t, d)


def _row(v):
    return v.astype(F32)[None, :]


def kernel(x, p, ssd_w_in, ssd_conv_w, ssd_conv_b, ssd_dt_bias, ssd_a_log, ssd_d, ssd_norm_g, ssd_w_out, moba_w_qkv, moba_w_out, conv_w_pw1, conv_b_pw1, conv_w_dw, conv_b_dw, conv_ln_g, conv_ln_b, conv_w_pw2, peer_w_q, peer_sub_keys, peer_u, peer_v, ln_mix_g, ln_mix_b, ln_ffn_g, ln_ffn_b, ple_w_gate, ple_w_proj):
    bsz, seq, d = x.shape
    t = bsz * seq
    x = x.reshape(t, d)
    zx_end = SSD_D_INNER + SSD_D_INNER + SSD_BC_DIM
    for i in range(DEPTH):
        kind, j = i % N_MIXERS, i // N_MIXERS
        g_mix, b_mix = _row(ln_mix_g[i]), _row(ln_mix_b[i])
        if kind == 0:
            w_in = ssd_w_in[j]
            w_dt = jnp.pad(w_in[:, zx_end:], ((0, 0), (0, LANES - SSD_N_HEADS)))
            zxbc = matmul(x, w_in[:, :zx_end].astype(BF16))
            dtp = matmul(x, w_dt.astype(BF16))
            y = ssd_core(zxbc, dtp, ssd_conv_w[j], ssd_conv_b[j], ssd_dt_bias[j], ssd_a_log[j], ssd_d[j],
                         ssd_norm_g[j], bsz, seq)
            x = matmul_ln(y, ssd_w_out[j].astype(BF16), x, g_mix, b_mix)
        elif kind == 1:
            qkv = matmul(x, moba_w_qkv[j].astype(BF16))
            att = moba_core(qkv, bsz, seq)
            x = matmul_ln(att, moba_w_out[j].astype(BF16), x, g_mix, b_mix)
        else:
            w1 = conv_w_pw1[j].astype(BF16)
            h = matmul_glu(x, w1[:, :D_MODEL], w1[:, D_MODEL:],
                           _row(conv_b_pw1[j][:D_MODEL]), _row(conv_b_pw1[j][D_MODEL:]))
            h = dwconv(h, conv_w_dw[j], conv_b_dw[j], bsz, seq)
            x = matmul_ln(h, conv_w_pw2[j].astype(BF16), x, g_mix, b_mix,
                          pre=(_row(conv_ln_g[j]), _row(conv_ln_b[j])))
        sk = peer_sub_keys[i].reshape(PEER_HEADS * 2, PEER_N_KEYS, PEER_HALF).astype(BF16)
        eidx, gates = peer_route(x, peer_w_q[i].astype(BF16), sk)
        x = peer_experts(x, eidx.T * WORD_ROWS, gates.T, pack_expert_table(peer_u[i]), pack_expert_table(peer_v[i]),
                         ln_ffn_g[i].astype(F32), ln_ffn_b[i].astype(F32))
        x = ple_add(x, p[i].reshape(t, -1), ple_w_gate[i].astype(BF16), ple_w_proj[i].astype(BF16))
    return x.reshape(bsz, seq, d)
```

```python
import functools
import math

import jax
import jax.numpy as jnp
from jax import lax
from jax.experimental import pallas as pl
from jax.experimental.pallas import tpu as pltpu

F32 = jnp.float32
BF16 = jnp.bfloat16

D_MODEL = 1024
DEPTH = 4
N_MIXERS = 3

SSD_D_INNER = 2048
SSD_HEADDIM = 64
SSD_N_HEADS = 32
SSD_N_GROUPS = 4
SSD_HEADS_PER_GROUP = 8
SSD_D_STATE = 128
SSD_CONV_WIDTH = 4
SSD_CHUNK = 128
SSD_GROUP_LANES = SSD_D_INNER // SSD_N_GROUPS
SSD_BC_DIM = 2 * SSD_N_GROUPS * SSD_D_STATE

MOBA_HEAD_DIM = 64
MOBA_N_HEADS = 16
MOBA_BLOCK = 256
MOBA_TOPK = 3
MOBA_Q_BLOCK = 128

CONV_KERNEL = 31
CONV_PAD_ROWS = 32

PEER_N_KEYS = 128
PEER_HEADS = 8
PEER_TOPK = 16
PEER_QUERY_DIM = 256
PEER_HALF = 128
PEER_PICKS = PEER_HEADS * PEER_TOPK

LN_EPS = 1e-5
DEEPNORM_ALPHA = (2 * DEPTH) ** 0.25

LANES = 128
VMEM_LIMIT = 48 * 1024 * 1024

NEG_INF = float("-inf")


def _params(sem):
    return pltpu.CompilerParams(dimension_semantics=sem, vmem_limit_bytes=VMEM_LIMIT)


def _layer_norm(x, g, b):
    mu = jnp.mean(x, axis=-1, keepdims=True)
    xc = x - mu
    var = jnp.mean(xc * xc, axis=-1, keepdims=True)
    return xc * lax.rsqrt(var + LN_EPS) * g + b


def _sigmoid(x):
    return 1.0 / (1.0 + jnp.exp(-x))


def _silu(x):
    return x * _sigmoid(x)


def _dot(a, b):
    return jnp.dot(a.astype(BF16), b.astype(BF16), preferred_element_type=F32)


def _dot_nt(a, b):
    return lax.dot_general(a.astype(BF16), b.astype(BF16), (((1,), (1,)), ((), ())),
                           preferred_element_type=F32)


def _dot_exact(a, b):
    return jnp.dot(a, b, preferred_element_type=F32, precision=lax.Precision.HIGHEST)


def _mm_kernel(a_ref, w_ref, o_ref):
    o_ref[...] = _dot(a_ref[...], w_ref[...])


def matmul(a, w, tm=1024, tn=512):
    m, k = a.shape
    n = w.shape[1]
    tm, tn = min(tm, m), min(tn, n)
    return pl.pallas_call(
        _mm_kernel,
        grid=(m // tm, n // tn),
        in_specs=[pl.BlockSpec((tm, k), lambda i, j: (i, 0)),
                  pl.BlockSpec((k, tn), lambda i, j: (0, j))],
        out_specs=pl.BlockSpec((tm, tn), lambda i, j: (i, j)),
        out_shape=jax.ShapeDtypeStruct((m, n), F32),
        name="mm",
        compiler_params=_params(("parallel", "parallel")),
    )(a, w)


def _mm_glu_kernel(a_ref, w1_ref, w2_ref, b1_ref, b2_ref, o_ref):
    a = a_ref[...].astype(BF16)
    h1 = jnp.dot(a, w1_ref[...], preferred_element_type=F32) + b1_ref[...]
    h2 = jnp.dot(a, w2_ref[...], preferred_element_type=F32) + b2_ref[...]
    o_ref[...] = h1 * _sigmoid(h2)


def matmul_glu(a, w1, w2, b1, b2, tm=512, tn=512):
    m, k = a.shape
    n = w1.shape[1]
    tm, tn = min(tm, m), min(tn, n)
    wspec = pl.BlockSpec((k, tn), lambda j, i: (0, j))
    bspec = pl.BlockSpec((1, tn), lambda j, i: (0, j))
    return pl.pallas_call(
        _mm_glu_kernel,
        grid=(n // tn, m // tm),
        in_specs=[pl.BlockSpec((tm, k), lambda j, i: (i, 0)), wspec, wspec, bspec, bspec],
        out_specs=pl.BlockSpec((tm, tn), lambda j, i: (i, j)),
        out_shape=jax.ShapeDtypeStruct((m, n), F32),
        name="mm_glu",
        compiler_params=_params(("parallel", "parallel")),
    )(a, w1, w2, b1, b2)


def _mm_ln_kernel(a_ref, w_ref, res_ref, g_ref, b_ref, o_ref):
    mix = _dot(a_ref[...], w_ref[...])
    o_ref[...] = _layer_norm(DEEPNORM_ALPHA * res_ref[...] + mix, g_ref[...], b_ref[...])


def _mm_lnsilu_ln_kernel(a_ref, pg_ref, pb_ref, w_ref, res_ref, g_ref, b_ref, o_ref):
    h = _silu(_layer_norm(a_ref[...], pg_ref[...], pb_ref[...]))
    mix = _dot(h, w_ref[...])
    o_ref[...] = _layer_norm(DEEPNORM_ALPHA * res_ref[...] + mix, g_ref[...], b_ref[...])


def matmul_ln(a, w, res, g, b, pre=None, tm=256):
    m, k = a.shape
    n = w.shape[1]
    tm = min(tm, m)
    row = lambda width: pl.BlockSpec((1, width), lambda i: (0, 0))
    a_spec = pl.BlockSpec((tm, k), lambda i: (i, 0))
    w_spec = pl.BlockSpec((k, n), lambda i: (0, 0))
    io_spec = pl.BlockSpec((tm, n), lambda i: (i, 0))
    if pre is None:
        body, ins, args = _mm_ln_kernel, [a_spec, w_spec, io_spec, row(n), row(n)], (a, w, res, g, b)
    else:
        body = _mm_lnsilu_ln_kernel
        ins = [a_spec, row(k), row(k), w_spec, io_spec, row(n), row(n)]
        args = (a, pre[0], pre[1], w, res, g, b)
    return pl.pallas_call(
        body, grid=(m // tm,), in_specs=ins, out_specs=io_spec,
        out_shape=jax.ShapeDtypeStruct((m, n), F32),
        name="mm_ln",
        compiler_params=_params(("parallel",)),
    )(*args)


def _ple_kernel(x_ref, p_ref, wg_ref, wp_ref, xres_ref, o_ref):
    gate = _sigmoid(_dot(x_ref[...], wg_ref[...]))
    o_ref[...] = xres_ref[...] + gate * _dot(p_ref[...], wp_ref[...])


def ple_add(x, p, wg, wp, tm=512, tn=512):
    m, k = x.shape
    kp = p.shape[1]
    n = wg.shape[1]
    tm, tn = min(tm, m), min(tn, n)
    return pl.pallas_call(
        _ple_kernel,
        grid=(n // tn, m // tm),
        in_specs=[pl.BlockSpec((tm, k), lambda j, i: (i, 0)),
                  pl.BlockSpec((tm, kp), lambda j, i: (i, 0)),
                  pl.BlockSpec((k, tn), lambda j, i: (0, j)),
                  pl.BlockSpec((kp, tn), lambda j, i: (0, j)),
                  pl.BlockSpec((tm, tn), lambda j, i: (i, j))],
        out_specs=pl.BlockSpec((tm, tn), lambda j, i: (i, j)),
        out_shape=jax.ShapeDtypeStruct((m, n), F32),
        name="ple_add",
        compiler_params=_params(("parallel", "parallel")),
    )(x, p, wg, wp, x)


def _ssd_kernel(z_ref, x_ref, bc_ref, dt_ref, cwx_ref, cwbc_ref, cbx_ref, cbbc_ref, dtb_ref, alog_ref,
                dexp_ref, ng_ref, expand_ref, o_ref, xpad, bcpad, state, yacc):
    L = SSD_CHUNK
    first = pl.program_id(1) == 0

    @pl.when(first)
    def _():
        xpad[0:8, :] = jnp.zeros((8, SSD_D_INNER), F32)
        bcpad[0:8, :] = jnp.zeros((8, SSD_BC_DIM), F32)
        state[...] = jnp.zeros_like(state)

    xpad[8:8 + L, :] = x_ref[...]
    bcpad[8:8 + L, :] = bc_ref[...]
    xs = cbx_ref[...]
    bc = cbbc_ref[...]
    for k in range(SSD_CONV_WIDTH):
        lo = 8 - (SSD_CONV_WIDTH - 1) + k
        xs = xs + cwx_ref[k:k + 1, :] * xpad[lo:lo + L, :]
        bc = bc + cwbc_ref[k:k + 1, :] * bcpad[lo:lo + L, :]
    xs = _silu(xs)
    bc = _silu(bc)
    xpad[0:8, :] = x_ref[L - 8:L, :]
    bcpad[0:8, :] = bc_ref[L - 8:L, :]

    lane = lax.broadcasted_iota(jnp.int32, (1, LANES), 1)
    dt_in = dt_ref[...] + dtb_ref[...]
    dt = jnp.maximum(dt_in, 0.0) + jnp.log1p(jnp.exp(-jnp.abs(dt_in)))
    dt = jnp.where(lane < SSD_N_HEADS, dt, 0.0)
    a = -jnp.exp(alog_ref[...])
    row = lax.broadcasted_iota(jnp.int32, (L, L), 0)
    col = lax.broadcasted_iota(jnp.int32, (L, L), 1)
    causal = col <= row
    a_cum = _dot_exact(causal.astype(F32), dt * a)
    a_cum_t = a_cum.T
    expand = expand_ref[...]
    dt_e = _dot_exact(dt, expand)
    acum_e = _dot_exact(a_cum, expand)
    alast_e = acum_e[L - 1:L, :]
    x_dt = xs * dt_e
    x_end = jnp.exp(alast_e - acum_e) * x_dt
    decay_in = jnp.exp(acum_e)

    lane_l = lax.broadcasted_iota(jnp.int32, (L, LANES), 1)
    for g in range(SSD_N_GROUPS):
        gl = slice(g * SSD_GROUP_LANES, (g + 1) * SSD_GROUP_LANES)
        b_g = bc[:, g * SSD_D_STATE:(g + 1) * SSD_D_STATE]
        c_g = bc[:, (SSD_N_GROUPS + g) * SSD_D_STATE:(SSD_N_GROUPS + g + 1) * SSD_D_STATE]
        cb = _dot_nt(c_g, b_g)
        prev = state[:, gl]
        yacc[:, gl] = _dot(c_g, prev) * decay_in[:, gl]
        new_states = _dot(b_g.T, x_end[:, gl])
        state[:, gl] = jnp.exp(alast_e[:, gl]) * prev + new_states
        for pr in range(SSD_HEADS_PER_GROUP // 2):
            ms = []
            for sub in range(2):
                h = g * SSD_HEADS_PER_GROUP + 2 * pr + sub
                seg = a_cum[:, h:h + 1] - a_cum_t[h:h + 1, :]
                ms.append(cb * jnp.exp(jnp.where(causal, seg, NEG_INF)))
            m_cat = jnp.concatenate(ms, axis=1)
            pl_ = slice(g * SSD_GROUP_LANES + pr * LANES, g * SSD_GROUP_LANES + (pr + 1) * LANES)
            xp = x_dt[:, pl_]
            x_bd = jnp.concatenate([jnp.where(lane_l < SSD_HEADDIM, xp, 0.0),
                                    jnp.where(lane_l >= SSD_HEADDIM, xp, 0.0)], axis=0)
            yacc[:, pl_] = yacc[:, pl_] + _dot(m_cat, x_bd)

    y = yacc[...] + dexp_ref[...] * xs
    gated = y * _silu(z_ref[...])
    for g in range(SSD_N_GROUPS):
        gl = slice(g * SSD_GROUP_LANES, (g + 1) * SSD_GROUP_LANES)
        gg = gated[:, gl]
        ms = jnp.mean(gg * gg, axis=-1, keepdims=True)
        o_ref[:, gl] = gg * lax.rsqrt(ms + LN_EPS) * ng_ref[:, gl]


def ssd_core(zxbc, dtp, conv_w, conv_b, dt_bias, a_log, d_skip, norm_g, bsz, seq):
    L = SSD_CHUNK
    nc = seq // L
    pad_h = LANES - SSD_N_HEADS
    head_of_lane = jnp.arange(SSD_D_INNER, dtype=jnp.int32) // SSD_HEADDIM
    expand = (jnp.arange(LANES, dtype=jnp.int32)[:, None] == head_of_lane[None, :]).astype(F32)
    dexp = jnp.repeat(d_skip.astype(F32), SSD_HEADDIM)[None, :]
    dtb = jnp.pad(dt_bias.astype(F32), (0, pad_h))[None, :]
    alog = jnp.pad(a_log.astype(F32), (0, pad_h))[None, :]
    cwx, cwbc = conv_w[:, :SSD_D_INNER], conv_w[:, SSD_D_INNER:]
    cbx, cbbc = conv_b[None, :SSD_D_INNER], conv_b[None, SSD_D_INNER:]
    chunk = lambda b, c: b * nc + c
    full = lambda shape: pl.BlockSpec(shape, lambda b, c: (0, 0))
    return pl.pallas_call(
        _ssd_kernel,
        grid=(bsz, nc),
        in_specs=[pl.BlockSpec((L, SSD_D_INNER), lambda b, c: (chunk(b, c), 0)),
                  pl.BlockSpec((L, SSD_D_INNER), lambda b, c: (chunk(b, c), 1)),
                  pl.BlockSpec((L, SSD_BC_DIM), lambda b, c: (chunk(b, c), 4)),
                  pl.BlockSpec((L, LANES), lambda b, c: (chunk(b, c), 0)),
                  full((SSD_CONV_WIDTH, SSD_D_INNER)), full((SSD_CONV_WIDTH, SSD_BC_DIM)),
                  full((1, SSD_D_INNER)), full((1, SSD_BC_DIM)),
                  full((1, LANES)), full((1, LANES)),
                  full((1, SSD_D_INNER)), full((1, SSD_D_INNER)),
                  full((LANES, SSD_D_INNER))],
        out_specs=pl.BlockSpec((L, SSD_D_INNER), lambda b, c: (chunk(b, c), 0)),
        out_shape=jax.ShapeDtypeStruct((bsz * seq, SSD_D_INNER), F32),
        scratch_shapes=[pltpu.VMEM((L + 8, SSD_D_INNER), F32),
                        pltpu.VMEM((L + 8, SSD_BC_DIM), F32),
                        pltpu.VMEM((SSD_D_STATE, SSD_D_INNER), F32),
                        pltpu.VMEM((L, SSD_D_INNER), F32)],
        name="ssd_core",
        compiler_params=_params(("parallel", "arbitrary")),
    )(zxbc, zxbc, zxbc, dtp, cwx, cwbc, cbx, cbbc, dtb, alog, dexp, norm_g[None, :], expand)


def _moba_kernel(slopes_ref, qt_ref, k_ref, vt_ref, o_ref, kmean, sel_buf, logit_buf, max_buf, sum_buf):
    hp = pl.program_id(1)
    qi = pl.program_id(2)
    Q, KB, DH = MOBA_Q_BLOCK, MOBA_BLOCK, MOBA_HEAD_DIM
    n_blk = k_ref.shape[0] // KB
    own = (qi * Q) // KB
    q0 = qi * Q
    scale = DH ** -0.5

    @pl.when(qi == 0)
    def _():
        for n in range(n_blk):
            kmean[n:n + 1, :] = jnp.mean(k_ref[n * KB:(n + 1) * KB, :], axis=0, keepdims=True)

    qt = qt_ref[0]
    feat = lax.broadcasted_iota(jnp.int32, (2 * DH, Q), 0)
    q_bd = jnp.concatenate([jnp.where(feat < DH, qt, 0.0), jnp.where(feat >= DH, qt, 0.0)], axis=1)
    q_bd = q_bd.astype(BF16)
    blk = lax.broadcasted_iota(jnp.int32, (n_blk, 2 * Q), 0)
    gate = jnp.where(blk < own, _dot(kmean[...], q_bd), NEG_INF)
    rank = jnp.zeros((n_blk, 2 * Q), jnp.int32)
    for m in range(n_blk):
        gm = gate[m:m + 1, :]
        beats = (gm > gate) | ((gm == gate) & (blk > m))
        rank = rank + beats.astype(jnp.int32)
    sel_buf[...] = ((rank < MOBA_TOPK) & (blk < own)).astype(F32)
    max_buf[...] = jnp.full(max_buf.shape, NEG_INF, F32)
    sum_buf[...] = jnp.zeros(sum_buf.shape, F32)
    slope = jnp.concatenate([jnp.full((1, Q), slopes_ref[2 * hp], F32),
                             jnp.full((1, Q), slopes_ref[2 * hp + 1], F32)], axis=1)
    qpos = lax.broadcasted_iota(jnp.int32, (KB, 2 * Q), 1) & (Q - 1)
    qk = qpos - lax.broadcasted_iota(jnp.int32, (KB, 2 * Q), 0)

    def fold(x):
        return x.reshape(KB // 8, 8, 2 * Q)

    n_trips = (own + 2) // 2

    def pass1(k, carry):
        for n in (2 * k, 2 * k + 1):
            start = pl.multiple_of(n * KB, KB)
            dist = qk + (q0 - n * KB)
            keep = (sel_buf[pl.ds(n, 1), :] > 0.0) | ((dist >= 0) & (n == own))
            lg = _dot(k_ref[pl.ds(start, KB), :], q_bd) * scale - slope * dist.astype(F32)
            lg = jnp.where(keep, lg, NEG_INF)
            logit_buf[n] = lg
            max_buf[...] = jnp.maximum(max_buf[...], jnp.max(fold(lg), axis=0))
        return carry

    lax.fori_loop(0, n_trips, pass1, 0)
    col_max = jnp.max(max_buf[...], axis=0, keepdims=True)

    def pass2(k, accs):
        for n in (2 * k, 2 * k + 1):
            p = jnp.exp(logit_buf[n] - col_max)
            sum_buf[...] = sum_buf[...] + jnp.sum(fold(p), axis=0)
            vt = vt_ref[0, n]
            accs = tuple(accs[i] + _dot(vt[i * DH:(i + 1) * DH, :], p[:, i * Q:(i + 1) * Q]) for i in range(2))
        return accs

    accs = lax.fori_loop(0, n_trips, pass2, (jnp.zeros((DH, Q), F32), jnp.zeros((DH, Q), F32)))
    denom = jnp.sum(sum_buf[...], axis=0, keepdims=True)
    for i in range(2):
        o_ref[0, i * DH:(i + 1) * DH, :] = accs[i] / denom[:, i * Q:(i + 1) * Q]


def moba_core(qkv, bsz, seq):
    Q, KB = MOBA_Q_BLOCK, MOBA_BLOCK
    nq, n_blk = seq // Q, seq // KB
    n_pairs = MOBA_N_HEADS // 2
    slopes = 2.0 ** (-8.0 * jnp.arange(1, MOBA_N_HEADS + 1, dtype=F32) / MOBA_N_HEADS)
    q_t = qkv[:, :D_MODEL].reshape(bsz, seq, D_MODEL).transpose(0, 2, 1)
    v_t = qkv[:, 2 * D_MODEL:].reshape(bsz, n_blk, KB, D_MODEL).transpose(0, 1, 3, 2)
    grid_spec = pltpu.PrefetchScalarGridSpec(
        num_scalar_prefetch=1,
        grid=(bsz, n_pairs, nq),
        in_specs=[pl.BlockSpec((1, LANES, Q), lambda b, h, i, s: (b, h, i)),
                  pl.BlockSpec((seq, LANES), lambda b, h, i, s: (b, n_pairs + h)),
                  pl.BlockSpec((1, n_blk, LANES, KB), lambda b, h, i, s: (b, 0, h, 0))],
        out_specs=pl.BlockSpec((1, LANES, Q), lambda b, h, i, s: (b, h, i)),
        scratch_shapes=[pltpu.VMEM((n_blk, LANES), F32),
                        pltpu.VMEM((n_blk, 2 * Q), F32),
                        pltpu.VMEM((n_blk, KB, 2 * Q), F32),
                        pltpu.VMEM((8, 2 * Q), F32), pltpu.VMEM((8, 2 * Q), F32)],
    )
    out_t = pl.pallas_call(
        _moba_kernel, grid_spec=grid_spec,
        out_shape=jax.ShapeDtypeStruct((bsz, D_MODEL, seq), F32),
        name="moba_core",
        compiler_params=_params(("parallel", "parallel", "arbitrary")),
    )(slopes, q_t, qkv, v_t)
    return out_t.transpose(0, 2, 1).reshape(bsz * seq, D_MODEL)


CONV_ROWS = 256


def _dwconv_kernel(h_ref, w_ref, b_ref, o_ref, pad):
    seq = h_ref.shape[1]
    pad[0:CONV_PAD_ROWS, :] = jnp.zeros((CONV_PAD_ROWS, pad.shape[1]), F32)
    pad[CONV_PAD_ROWS:CONV_PAD_ROWS + seq, :] = h_ref[0]
    first = CONV_PAD_ROWS - (CONV_KERNEL - 1)
    for c in range(seq // CONV_ROWS):
        r0 = c * CONV_ROWS
        acc = jnp.broadcast_to(b_ref[...], (CONV_ROWS, pad.shape[1]))
        for k in range(CONV_KERNEL):
            acc = acc + w_ref[k:k + 1, :] * pad[r0 + first + k:r0 + first + k + CONV_ROWS, :]
        o_ref[0, r0:r0 + CONV_ROWS, :] = acc


def dwconv(h, w, b, bsz, seq, tc=256):
    chans = h.shape[1]
    h3 = h.reshape(bsz, seq, chans)
    out = pl.pallas_call(
        _dwconv_kernel,
        grid=(bsz, chans // tc),
        in_specs=[pl.BlockSpec((1, seq, tc), lambda b, c: (b, 0, c)),
                  pl.BlockSpec((CONV_KERNEL, tc), lambda b, c: (0, c)),
                  pl.BlockSpec((1, tc), lambda b, c: (0, c))],
        out_specs=pl.BlockSpec((1, seq, tc), lambda b, c: (b, 0, c)),
        out_shape=jax.ShapeDtypeStruct((bsz, seq, chans), F32),
        scratch_shapes=[pltpu.VMEM((CONV_PAD_ROWS + seq, tc), F32)],
        name="dwconv",
        compiler_params=_params(("parallel", "parallel")),
    )(h3, w, b[None, :])
    return out.reshape(bsz * seq, chans)


PEER_CAND_PER_A = tuple(PEER_TOPK // (a + 1) for a in range(PEER_TOPK))
PEER_CAND_ROWS = 56


def _top16_rows(cur, dst_val, dst_idx, payload=None):
    rows = cur.shape[0]
    riota = lax.broadcasted_iota(jnp.int32, cur.shape, 0).astype(F32)
    for i in range(PEER_TOPK):
        m = jnp.max(cur, axis=0, keepdims=True)
        am = jnp.min(jnp.where(cur == m, riota, float(rows)), axis=0, keepdims=True)
        hit = riota == am
        dst_val[i:i + 1, :] = m
        if payload is None:
            dst_idx[i:i + 1, :] = am
        else:
            dst_idx[i:i + 1, :] = jnp.max(jnp.where(hit, payload, -1.0), axis=0, keepdims=True)
        cur = jnp.where(hit, NEG_INF, cur)


def _peer_route_kernel(x_ref, wq_ref, sk_ref, eidx_ref, gate_ref, v1, i1, v2, i2, cand, cidx, best, bidx):
    K = PEER_TOPK
    q = _dot(x_ref[...], wq_ref[...])
    cand[PEER_CAND_ROWS - 8:PEER_CAND_ROWS, :] = jnp.full((8, cand.shape[1]), NEG_INF, F32)
    cidx[PEER_CAND_ROWS - 8:PEER_CAND_ROWS, :] = jnp.zeros((8, cand.shape[1]), F32)
    for h in range(PEER_HEADS):
        for c, (vv, ii) in enumerate(((v1, i1), (v2, i2))):
            hc = 2 * h + c
            s_t = _dot_nt(sk_ref[hc], q[:, hc * PEER_HALF:(hc + 1) * PEER_HALF])
            _top16_rows(s_t, vv, ii)
        off = 0
        for a, n_b in enumerate(PEER_CAND_PER_A):
            cand[off:off + n_b, :] = v1[a:a + 1, :] + v2[0:n_b, :]
            cidx[off:off + n_b, :] = i1[a:a + 1, :] * float(PEER_N_KEYS) + i2[0:n_b, :]
            off += n_b
        _top16_rows(cand[...], best, bidx, payload=cidx[...])
        eidx_ref[h * K:(h + 1) * K, :] = bidx[...].astype(jnp.int32)
        b = best[...]
        e = jnp.exp(b - b[0:1, :])
        gate_ref[h * K:(h + 1) * K, :] = e / jnp.sum(e, axis=0, keepdims=True)


def peer_route(x, wq, sk, tm=256):
    t, d = x.shape
    tm = min(tm, t)
    K = PEER_TOPK
    out_spec = pl.BlockSpec((PEER_PICKS, tm), lambda i: (0, i))
    return pl.pallas_call(
        _peer_route_kernel,
        grid=(t // tm,),
        in_specs=[pl.BlockSpec((tm, d), lambda i: (i, 0)),
                  pl.BlockSpec(wq.shape, lambda i: (0, 0)),
                  pl.BlockSpec(sk.shape, lambda i: (0, 0, 0))],
        out_specs=[out_spec, out_spec],
        out_shape=[jax.ShapeDtypeStruct((PEER_PICKS, t), jnp.int32),
                   jax.ShapeDtypeStruct((PEER_PICKS, t), F32)],
        scratch_shapes=[pltpu.VMEM((K, tm), F32)] * 4
                       + [pltpu.VMEM((PEER_CAND_ROWS, tm), F32)] * 2
                       + [pltpu.VMEM((K, tm), F32)] * 2,
        name="peer_route",
        compiler_params=_params(("parallel",)),
    )(x, wq, sk)


PEER_STEP_TOKENS = 64
PEER_HALF_TOKENS = PEER_STEP_TOKENS // 2
CHUNKS = D_MODEL // LANES
WORD_ROWS = CHUNKS // 2
PEER_VMEM_LIMIT = 44 * 1024 * 1024


def _gelu_tanh(x):
    return 0.5 * x * (1.0 + jnp.tanh(math.sqrt(2.0 / math.pi) * (x + 0.044715 * (x * x * x))))


def pack_expert_table(w):
    n = w.shape[0]
    bits = lax.bitcast_convert_type(w.astype(jnp.bfloat16), jnp.uint16).astype(jnp.uint32)
    bits = bits.reshape(n, WORD_ROWS, 2, LANES)
    words = lax.bitcast_convert_type(bits[:, :, 0, :] | (bits[:, :, 1, :] << 16), jnp.int32)
    return words.reshape(n * WORD_ROWS, LANES)


def _load_table(tab_hbm, tab, sem):
    @pl.when(pl.program_id(0) == 0)
    def _():
        cp = pltpu.make_async_copy(tab_hbm, tab, sem)
        cp.start()
        cp.wait()


def _gather_rows(ids, t, tab, stage):
    for j in range(PEER_PICKS):
        stage[j * WORD_ROWS:(j + 1) * WORD_ROWS, :] = tab[pl.ds(pl.multiple_of(ids[t, j], WORD_ROWS), WORD_ROWS), :]


def _chunk_mask():
    sub = lax.broadcasted_iota(jnp.int32, (CHUNKS, PEER_PICKS * CHUNKS), 0)
    lane = lax.broadcasted_iota(jnp.int32, (CHUNKS, PEER_PICKS * CHUNKS), 1)
    return sub == (lane & (CHUNKS - 1))


def _for_each_token(idx_hbm, ibuf, isem, tab, stages, finish):
    i = pl.program_id(0)
    n = pl.num_programs(0)
    H = PEER_HALF_TOKENS

    def ids_copy(half, bank):
        return pltpu.make_async_copy(idx_hbm.at[pl.ds(half * H, H)], ibuf.at[bank], isem.at[bank])

    @pl.when(i == 0)
    def _():
        ids_copy(0, 0).start()

    ids_copy(2 * i, 0).wait()
    ids_copy(2 * i + 1, 1).start()
    _gather_rows(ibuf.at[0], 0, tab, stages[0])
    for t in range(PEER_STEP_TOKENS):
        nxt = min(t + 1, PEER_STEP_TOKENS - 1)
        if t == H - 1:
            ids_copy(2 * i + 1, 1).wait()

            @pl.when(i + 1 < n)
            def _():
                ids_copy(2 * i + 2, 0).start()
        _gather_rows(ibuf.at[nxt // H], nxt % H, tab, stages[(t + 1) % 2])
        finish(t, pltpu.bitcast(stages[t % 2][...], jnp.bfloat16))


def _peer_act_kernel(idx_hbm, x_ref, gate_ref, tab_hbm, w_ref, tab, stage0, stage1, acts, sem, ibuf, isem):
    _load_table(tab_hbm, tab, sem)
    mask = _chunk_mask()

    def finish(t, rows):
        r = _dot_nt(x_ref[t], rows)
        acts[t:t + 1, :] = jnp.sum(jnp.where(mask, r, 0.0), axis=0, keepdims=True)

    _for_each_token(idx_hbm, ibuf, isem, tab, (stage0, stage1), finish)
    r_id = lax.broadcasted_iota(jnp.int32, (LANES, LANES), 0) // CHUNKS
    c_id = lax.broadcasted_iota(jnp.int32, (LANES, LANES), 1) // CHUNKS
    group_ones = (r_id == c_id).astype(F32)
    for c in range(CHUNKS):
        sl = (slice(None), slice(c * LANES, (c + 1) * LANES))
        w_ref[sl] = gate_ref[sl] * _gelu_tanh(_dot_exact(acts[sl], group_ones))


def _peer_mix_kernel(idx_hbm, w_ref, x_ref, g_ref, b_ref, tab_hbm, o_ref, tab, stage0, stage1, sem, ibuf, isem):
    _load_table(tab_hbm, tab, sem)
    mask = _chunk_mask()

    def finish(t, rows):
        wrow = jnp.broadcast_to(w_ref[t:t + 1, :], mask.shape)
        o_ref[t] = DEEPNORM_ALPHA * x_ref[t] + _dot(jnp.where(mask, wrow, 0.0), rows)

    _for_each_token(idx_hbm, ibuf, isem, tab, (stage0, stage1), finish)
    y = o_ref[...]
    mu = jnp.mean(jnp.mean(y, axis=2, keepdims=True), axis=1, keepdims=True)
    yc = y - mu
    var = jnp.mean(jnp.mean(yc * yc, axis=2, keepdims=True), axis=1, keepdims=True)
    o_ref[...] = yc * lax.rsqrt(var + LN_EPS) * g_ref[...][None] + b_ref[...][None]


def peer_experts(x, eidx_t, gates_t, u_tab, v_tab, g, b):
    t, d = x.shape
    TB, P = PEER_STEP_TOKENS, PEER_PICKS
    n = t // TB
    x3 = x.reshape(t, CHUNKS, LANES)
    gate_rep = jnp.repeat(gates_t, CHUNKS, axis=1)
    x_spec = pl.BlockSpec((TB, CHUNKS, LANES), lambda i: (i, 0, 0))
    wide_spec = pl.BlockSpec((TB, P * CHUNKS), lambda i: (i, 0))
    chunk_row = pl.BlockSpec((CHUNKS, LANES), lambda i: (0, 0))
    hbm = pl.BlockSpec(memory_space=pl.ANY)
    scratch = [pltpu.VMEM(u_tab.shape, jnp.int32),
               pltpu.VMEM((P * WORD_ROWS, LANES), jnp.int32), pltpu.VMEM((P * WORD_ROWS, LANES), jnp.int32)]
    ids_scratch = [pltpu.SMEM((2, PEER_HALF_TOKENS, P), jnp.int32), pltpu.SemaphoreType.DMA((2,))]
    params = pltpu.CompilerParams(dimension_semantics=("arbitrary",), vmem_limit_bytes=PEER_VMEM_LIMIT)
    w = pl.pallas_call(
        _peer_act_kernel,
        grid=(n,),
        in_specs=[hbm, x_spec, wide_spec, hbm],
        out_specs=wide_spec,
        out_shape=jax.ShapeDtypeStruct((t, P * CHUNKS), F32),
        scratch_shapes=scratch + [pltpu.VMEM((TB, P * CHUNKS), F32), pltpu.SemaphoreType.DMA(())] + ids_scratch,
        name="peer_act",
        compiler_params=params,
    )(eidx_t, x3, gate_rep, u_tab)
    out = pl.pallas_call(
        _peer_mix_kernel,
        grid=(n,),
        in_specs=[hbm, wide_spec, x_spec, chunk_row, chunk_row, hbm],
        out_specs=x_spec,
        out_shape=jax.ShapeDtypeStruct((t, CHUNKS, LANES), F32),
        scratch_shapes=scratch + [pltpu.SemaphoreType.DMA(())] + ids_scratch,
        name="peer_mix",
        compiler_params=params,
    )(eidx_t, w, x3, g.reshape(CHUNKS, LANES), b.reshape(CHUNKS, LANES), v_tab)
    return out.reshape(t, d)


def _row(v):
    return v.astype(F32)[None, :]


def kernel(x, p, ssd_w_in, ssd_conv_w, ssd_conv_b, ssd_dt_bias, ssd_a_log, ssd_d, ssd_norm_g, ssd_w_out, moba_w_qkv, moba_w_out, conv_w_pw1, conv_b_pw1, conv_w_dw, conv_b_dw, conv_ln_g, conv_ln_b, conv_w_pw2, peer_w_q, peer_sub_keys, peer_u, peer_v, ln_mix_g, ln_mix_b, ln_ffn_g, ln_ffn_b, ple_w_gate, ple_w_proj):
    bsz, seq, d = x.shape
    t = bsz * seq
    x = x.reshape(t, d)
    zx_end = SSD_D_INNER + SSD_D_INNER + SSD_BC_DIM
    for i in range(DEPTH):
        kind, j = i % N_MIXERS, i // N_MIXERS
        g_mix, b_mix = _row(ln_mix_g[i]), _row(ln_mix_b[i])
        if kind == 0:
            w_in = ssd_w_in[j]
            w_dt = jnp.pad(w_in[:, zx_end:], ((0, 0), (0, LANES - SSD_N_HEADS)))
            zxbc = matmul(x, w_in[:, :zx_end].astype(BF16))
            dtp = matmul(x, w_dt.astype(BF16))
            y = ssd_core(zxbc, dtp, ssd_conv_w[j], ssd_conv_b[j], ssd_dt_bias[j], ssd_a_log[j], ssd_d[j],
                         ssd_norm_g[j], bsz, seq)
            x = matmul_ln(y, ssd_w_out[j].astype(BF16), x, g_mix, b_mix)
        elif kind == 1:
            qkv = matmul(x, moba_w_qkv[j].astype(BF16))
            att = moba_core(qkv, bsz, seq)
            x = matmul_ln(att, moba_w_out[j].astype(BF16), x, g_mix, b_mix)
        else:
            w1 = conv_w_pw1[j].astype(BF16)
            h = matmul_glu(x, w1[:, :D_MODEL], w1[:, D_MODEL:],
                           _row(conv_b_pw1[j][:D_MODEL]), _row(conv_b_pw1[j][D_MODEL:]))
            h = dwconv(h, conv_w_dw[j], conv_b_dw[j], bsz, seq)
            x = matmul_ln(h, conv_w_pw2[j].astype(BF16), x, g_mix, b_mix,
                          pre=(_row(conv_ln_g[j]), _row(conv_ln_b[j])))
        sk = peer_sub_keys[i].reshape(PEER_HEADS * 2, PEER_N_KEYS, PEER_HALF).astype(BF16)
        eidx, gates = peer_route(x, peer_w_q[i].astype(BF16), sk)
        x = peer_experts(x, eidx.T * WORD_ROWS, gates.T, pack_expert_table(peer_u[i]), pack_expert_table(peer_v[i]),
                         ln_ffn_g[i].astype(F32), ln_ffn_b[i].astype(F32))
        x = ple_add(x, p[i].reshape(t, -1), ple_w_gate[i].astype(BF16), ple_w_proj[i].astype(BF16))
    return x.reshape(bsz, seq, d)
```
